```python
import math
import jax
import jax.numpy as jnp
from jax import lax
import numpy as np

D_MODEL = 1024
BATCH = 16
SEQ = 4096
DEPTH = 1
DEC_BATCH = 16
DEC_SEQ = 32
PAST_LEN = 1024

CHUNK = 64
N_HEADS = D_MODEL // 128
HEAD_DIM = 64
V_DIM = 2 * HEAD_DIM
QK_WIDTH = N_HEADS * 2 * HEAD_DIM
V_WIDTH = N_HEADS * V_DIM
ROT_DIM = HEAD_DIM // 4
ROPE_THETA = 500000.0
ATTN_SCALE = HEAD_DIM ** -0.5
Q_BLOCK = 128
POOL_WINDOWS = (2, 4, 8, 16)
N_POOL_GROUPS = 4
POOL_WIDTH = D_MODEL // 2
POOL_GROUP_DIM = POOL_WIDTH // N_POOL_GROUPS
POOL_BUF = max(POOL_WINDOWS) - 1
N_EXPERT_GROUPS = 4
EXPERTS_PER_GROUP = 4
N_EXPERTS = N_EXPERT_GROUPS * EXPERTS_PER_GROUP
EXPERT_TOP_K = 2
EXPERT_HIDDEN = D_MODEL // 2
IN_WIDTH = POOL_WIDTH + 2 * QK_WIDTH + V_WIDTH + 2 * D_MODEL
IN_SPLITS = (POOL_WIDTH, POOL_WIDTH + QK_WIDTH, POOL_WIDTH + 2 * QK_WIDTH, POOL_WIDTH + 2 * QK_WIDTH + V_WIDTH)
RMS_EPS = 1e-6

kernel_name = 'hybrid_pool_diffattn_hiermoe_stream_step'


def rms_norm(x, g):
    xf = x.astype(jnp.float32)
    xf = xf * lax.rsqrt(jnp.mean(xf * xf, axis=-1, keepdims=True) + RMS_EPS)
    return xf.astype(x.dtype) * g


def rope_partial(x, pos):
    inv_freq = jnp.power(ROPE_THETA, -jnp.arange(0, ROT_DIM, 2, dtype=jnp.float32) / ROT_DIM)
    ang = pos.astype(jnp.float32)[:, None] * inv_freq[None, :]
    cos = jnp.cos(ang)[:, None, None, :]
    sin = jnp.sin(ang)[:, None, None, :]
    xr = x[..., :ROT_DIM].astype(jnp.float32)
    x1, x2 = xr[..., :ROT_DIM // 2], xr[..., ROT_DIM // 2:]
    rot = jnp.concatenate([x1 * cos - x2 * sin, x2 * cos + x1 * sin], axis=-1).astype(x.dtype)
    return jnp.concatenate([rot, x[..., ROT_DIM:]], axis=-1)


def pool_mixer(u, buf, pos, pool_w, pool_scale):
    B, T, _ = u.shape
    ext = jnp.concatenate([buf, u], axis=1)
    cs = jnp.cumsum(ext.astype(jnp.float32), axis=1)
    cs = jnp.concatenate([jnp.zeros_like(cs[:, :1]), cs], axis=1)
    end = cs[:, POOL_BUF + 1:]
    uf = u.astype(jnp.float32)
    outs = []
    for g, w in enumerate(POOL_WINDOWS):
        sl = slice(g * POOL_GROUP_DIM, (g + 1) * POOL_GROUP_DIM)
        window_sum = end[..., sl] - cs[:, POOL_BUF + 1 - w:POOL_BUF + 1 - w + T, sl]
        count = jnp.minimum(w, pos + 1).astype(jnp.float32)[None, :, None]
        outs.append(window_sum / count - uf[..., sl])
    d = jnp.stack(outs, axis=2).astype(u.dtype)
    y = jnp.einsum('btgc,gce->btge', d, pool_w).reshape(B, T, POOL_WIDTH) * pool_scale
    return y, ext[:, -POOL_BUF:]


def diff_attend(q, k, v, mask, lam):
    s = jnp.einsum('bqhmd,bkhmd->bhmqk', q, k).astype(jnp.float32) * ATTN_SCALE
    s = jnp.where(mask, s, -jnp.inf)
    p = jax.nn.softmax(s, axis=-1)
    a = p[:, :, 0] - lam * p[:, :, 1]
    return jnp.einsum('bhqk,bkhv->bqhv', a.astype(v.dtype), v)


def diff_attn_prompt(q, k, v, lam):
    B, T = q.shape[0], q.shape[1]
    nb = T // Q_BLOCK
    q_blocks = jnp.moveaxis(q.reshape(B, nb, Q_BLOCK, N_HEADS, 2, HEAD_DIM), 1, 0)
    k_chunk = jnp.arange(T) // CHUNK

    def block(args):
        q_blk, start = args
        q_chunk = (start + jnp.arange(Q_BLOCK)) // CHUNK
        mask = k_chunk[None, :] <= q_chunk[:, None]
        return diff_attend(q_blk, k, v, mask, lam)

    out = lax.map(block, (q_blocks, jnp.arange(nb) * Q_BLOCK))
    return jnp.moveaxis(out, 0, 1).reshape(B, T, N_HEADS, V_DIM)


def hier_moe(h, w_router_group, b_router_group, w_router_expert, b_router_expert,
             w_expert_gate, w_expert_up, w_expert_down):
    B, T, _ = h.shape
    t = h.reshape(B * T, D_MODEL)
    g_logits = (t @ w_router_group).astype(jnp.float32) + b_router_group
    g_prob = jax.nn.softmax(g_logits, axis=-1)
    g_sel = jnp.argmax(g_logits, axis=-1)
    g_w = jnp.take_along_axis(g_prob, g_sel[:, None], axis=-1)
    e_logits_all = jnp.einsum('nd,gde->nge', t, w_router_expert).astype(jnp.float32) + b_router_expert
    e_logits = jnp.take_along_axis(e_logits_all, g_sel[:, None, None], axis=1)[:, 0]
    top_v, top_i = lax.top_k(e_logits, EXPERT_TOP_K)
    top_w = jax.nn.softmax(top_v, axis=-1) * g_w
    expert_id = g_sel[:, None] * EXPERTS_PER_GROUP + top_i
    combine = jnp.sum(jax.nn.one_hot(expert_id, N_EXPERTS, dtype=jnp.float32) * top_w[..., None], axis=1)
    y = jnp.zeros((B * T, D_MODEL), jnp.float32)
    for e in range(N_EXPERTS):
        hid = jax.nn.silu(t @ w_expert_gate[e]) * (t @ w_expert_up[e])
        y = y + combine[:, e:e + 1] * (hid @ w_expert_down[e]).astype(jnp.float32)
    return y.astype(h.dtype).reshape(B, T, D_MODEL)


def hybrid_layer(x, pos, pool_buf, k_past, v_past, lam_init,
                 norm_mix_g, w_in, b_gate, lambda_q1, lambda_k1, lambda_q2, lambda_k2, subln_g,
                 pool_w, pool_scale, w_pool_branch, w_attn_branch, w_out, norm_ffn_g,
                 w_router_group, b_router_group, w_router_expert, b_router_expert,
                 w_expert_gate, w_expert_up, w_expert_down):
    B, T, _ = x.shape
    h = rms_norm(x, norm_mix_g)
    z = h @ w_in
    u_pool, q, k, v, gate_logits = jnp.split(z, IN_SPLITS, axis=-1)
    q = rope_partial(q.reshape(B, T, N_HEADS, 2, HEAD_DIM), pos)
    k = rope_partial(k.reshape(B, T, N_HEADS, 2, HEAD_DIM), pos)
    v = v.reshape(B, T, N_HEADS, V_DIM)

    if pool_buf is None:
        pool_buf = jnp.zeros((B, POOL_BUF, POOL_WIDTH), u_pool.dtype)
    pool_y, new_pool = pool_mixer(u_pool, pool_buf, pos, pool_w, pool_scale)

    lam = (jnp.exp(jnp.sum(lambda_q1.astype(jnp.float32) * lambda_k1.astype(jnp.float32)))
           - jnp.exp(jnp.sum(lambda_q2.astype(jnp.float32) * lambda_k2.astype(jnp.float32))) + lam_init)
    if k_past is None:
        attn = diff_attn_prompt(q, k, v, lam)
    else:
        n_past = k_past.shape[1]
        k_all = jnp.concatenate([k_past.reshape(B, n_past, N_HEADS, 2, HEAD_DIM), k], axis=1)
        v_all = jnp.concatenate([v_past, v], axis=1)
        k_pos = jnp.concatenate([jnp.arange(n_past), pos])
        mask = (k_pos // CHUNK)[None, :] <= (pos // CHUNK)[:, None]
        attn = diff_attend(q, k_all, v_all, mask, lam)
    attn = (rms_norm(attn, subln_g) * (1.0 - lam_init)).reshape(B, T, V_WIDTH)

    g_pool = jax.nn.sigmoid(gate_logits[..., :D_MODEL] + b_gate[:D_MODEL])
    g_attn = jax.nn.sigmoid(gate_logits[..., D_MODEL:] + b_gate[D_MODEL:])
    merged = g_pool * (pool_y @ w_pool_branch) + g_attn * (attn @ w_attn_branch)
    x = x + merged @ w_out

    x = x + hier_moe(rms_norm(x, norm_ffn_g), w_router_group, b_router_group, w_router_expert,
                     b_router_expert, w_expert_gate, w_expert_up, w_expert_down)
    return x, k.reshape(B, T, N_HEADS, 2 * HEAD_DIM), v, new_pool


def setup_inputs(seed: int = 0) -> dict:
    key = jax.random.key(seed)
    ks = jax.random.split(key, 32)

    def nrm(k, shape, scale):
        return jax.random.normal(k, shape, jnp.float32) * scale

    def gain(k, shape):
        return 1.0 + 0.05 * jax.random.normal(k, shape, jnp.float32)

    L = DEPTH
    return {
        'x_prompt': nrm(ks[0], (BATCH, SEQ, D_MODEL), 1.0),
        'x_sample': nrm(ks[1], (DEC_BATCH, DEC_SEQ, D_MODEL), 1.0),
        'cache_k': nrm(ks[2], (L, DEC_BATCH, PAST_LEN, N_HEADS, 2 * HEAD_DIM), 1.0),
        'cache_v': nrm(ks[3], (L, DEC_BATCH, PAST_LEN, N_HEADS, V_DIM), 1.0),
        'state_pool': nrm(ks[4], (L, DEC_BATCH, POOL_BUF, POOL_WIDTH), 1.0),
        'norm_mix_g': gain(ks[5], (L, D_MODEL)),
        'w_in': nrm(ks[6], (L, D_MODEL, IN_WIDTH), D_MODEL ** -0.5),
        'b_gate': nrm(ks[7], (L, 2 * D_MODEL), 0.1),
        'lambda_q1': nrm(ks[8], (L, HEAD_DIM), 0.1),
        'lambda_k1': nrm(ks[9], (L, HEAD_DIM), 0.1),
        'lambda_q2': nrm(ks[10], (L, HEAD_DIM), 0.1),
        'lambda_k2': nrm(ks[11], (L, HEAD_DIM), 0.1),
        'subln_g': gain(ks[12], (L, V_DIM)),
        'pool_w': nrm(ks[13], (L, N_POOL_GROUPS, POOL_GROUP_DIM, POOL_GROUP_DIM), POOL_GROUP_DIM ** -0.5),
        'pool_scale': gain(ks[14], (L, POOL_WIDTH)),
        'w_pool_branch': nrm(ks[15], (L, POOL_WIDTH, D_MODEL), POOL_WIDTH ** -0.5),
        'w_attn_branch': nrm(ks[16], (L, V_WIDTH, D_MODEL), V_WIDTH ** -0.5),
        'w_out': nrm(ks[17], (L, D_MODEL, D_MODEL), D_MODEL ** -0.5),
        'norm_ffn_g': gain(ks[18], (L, D_MODEL)),
        'w_router_group': nrm(ks[19], (L, D_MODEL, N_EXPERT_GROUPS), D_MODEL ** -0.5),
        'b_router_group': nrm(ks[20], (L, N_EXPERT_GROUPS), 0.01),
        'w_router_expert': nrm(ks[21], (L, N_EXPERT_GROUPS, D_MODEL, EXPERTS_PER_GROUP), D_MODEL ** -0.5),
        'b_router_expert': nrm(ks[22], (L, N_EXPERT_GROUPS, EXPERTS_PER_GROUP), 0.01),
        'w_expert_gate': nrm(ks[23], (L, N_EXPERTS, D_MODEL, EXPERT_HIDDEN), D_MODEL ** -0.5),
        'w_expert_up': nrm(ks[24], (L, N_EXPERTS, D_MODEL, EXPERT_HIDDEN), D_MODEL ** -0.5),
        'w_expert_down': nrm(ks[25], (L, N_EXPERTS, EXPERT_HIDDEN, D_MODEL), EXPERT_HIDDEN ** -0.5),
        'final_norm_g': gain(ks[26], (D_MODEL,)),
    }


def reference(x_prompt, x_sample, cache_k, cache_v, state_pool,
              norm_mix_g, w_in, b_gate, lambda_q1, lambda_k1, lambda_q2, lambda_k2, subln_g,
              pool_w, pool_scale, w_pool_branch, w_attn_branch, w_out, norm_ffn_g,
              w_router_group, b_router_group, w_router_expert, b_router_expert,
              w_expert_gate, w_expert_up, w_expert_down, final_norm_g):
    pos_prompt = jnp.arange(x_prompt.shape[1])
    pos_sample = cache_k.shape[2] + jnp.arange(x_sample.shape[1])
    xp, xs = x_prompt, x_sample
    k_p, v_p, pool_p, k_s, v_s, pool_s = [], [], [], [], [], []
    for l in range(DEPTH):
        lam_init = 0.8 - 0.6 * math.exp(-0.3 * l)
        lp = (norm_mix_g[l], w_in[l], b_gate[l], lambda_q1[l], lambda_k1[l], lambda_q2[l], lambda_k2[l],
              subln_g[l], pool_w[l], pool_scale[l], w_pool_branch[l], w_attn_branch[l], w_out[l],
              norm_ffn_g[l], w_router_group[l], b_router_group[l], w_router_expert[l], b_router_expert[l],
              w_expert_gate[l], w_expert_up[l], w_expert_down[l])
        xp, kp, vp, pp = hybrid_layer(xp, pos_prompt, None, None, None, lam_init, *lp)
        xs, kss, vss, pss = hybrid_layer(xs, pos_sample, state_pool[l], cache_k[l], cache_v[l], lam_init, *lp)
        k_p.append(kp); v_p.append(vp); pool_p.append(pp)
        k_s.append(kss); v_s.append(vss); pool_s.append(pss)
    y_prompt = rms_norm(xp, final_norm_g)
    y_sample = rms_norm(xs, final_norm_g)
    new_k_prompt = jnp.stack(k_p)
    new_v_prompt = jnp.stack(v_p)
    new_pool_prompt = jnp.stack(pool_p)
    new_k_sample = jnp.stack(k_s)
    new_v_sample = jnp.stack(v_s)
    new_pool_sample = jnp.stack(pool_s)
    return (y_prompt, y_sample, new_k_prompt, new_v_prompt, new_pool_prompt, new_k_sample, new_v_sample, new_pool_sample)
```

```python
import functools
import math

import jax
import jax.numpy as jnp
from jax import lax
from jax.experimental import pallas as pl
from jax.experimental.pallas import tpu as pltpu

CHUNK = 64
HEAD_DIM = 64
V_DIM = 2 * HEAD_DIM
ROT_DIM = HEAD_DIM // 4
ROPE_THETA = 500000.0
ATTN_SCALE = HEAD_DIM ** -0.5
POOL_WINDOWS = (2, 4, 8, 16)
POOL_GROUP_DIM = 128
POOL_BUF = max(POOL_WINDOWS) - 1
POOL_HDR = POOL_BUF + 1
N_EXPERT_GROUPS = 4
EXPERTS_PER_GROUP = 4
N_EXPERTS = N_EXPERT_GROUPS * EXPERTS_PER_GROUP
RMS_EPS = 1e-6
LAM_INIT = 0.8 - 0.6 * math.exp(-0.3 * 0)

LANES = 128
VMEM_LIMIT = 56 * 1024 * 1024
NEG_BIG = -1e30

F32 = jnp.float32
BF16 = jnp.bfloat16


def _params(semantics):
    return pltpu.CompilerParams(dimension_semantics=semantics, vmem_limit_bytes=VMEM_LIMIT)


def _const_spec(shape):
    nd = len(shape)
    return pl.BlockSpec(shape, lambda *_: (0,) * nd)


def _rms(x, g):
    ms = jnp.mean(x * x, axis=-1, keepdims=True)
    return (x * lax.rsqrt(ms + RMS_EPS)) * g


def _inproj_kernel(x_ref, g_ref, w_ref, cos_ref, sa_ref, sb_ref, pbuf_ref, pw_ref, ps_ref,
                   q_ref, k_ref, v_ref, py_ref, npool_ref, ext_ref, carry_ref, *, pos0, n_heads):
    t = pl.program_id(0)
    b = pl.program_id(1)
    tm = x_ref.shape[0]
    pool_w = ext_ref.shape[1]
    qk_w = n_heads * 2 * HEAD_DIM

    h = _rms(x_ref[...], g_ref[...]).astype(BF16)

    @pl.when(t == 0)
    def _():
        ext_ref[0:POOL_HDR, :] = pbuf_ref[...]

    @pl.when(t > 0)
    def _():
        ext_ref[0:POOL_HDR, :] = carry_ref[b]

    ext_ref[POOL_HDR:, :] = jnp.dot(h, w_ref[:, 0:pool_w], preferred_element_type=F32)
    tail = ext_ref[tm:tm + POOL_HDR, :]
    carry_ref[b] = tail
    npool_ref[...] = tail

    cos = cos_ref[...]
    sa = sa_ref[...]
    sb = sb_ref[...]
    col_chunk = 4 * LANES
    for which, out_ref in ((0, q_ref), (1, k_ref)):
        for c in range(0, qk_w, col_chunk):
            base = pool_w + which * qk_w + c
            z = jnp.dot(h, w_ref[:, base:base + col_chunk], preferred_element_type=F32)
            for j in range(0, col_chunk, LANES):
                zc = z[:, j:j + LANES]
                rot = (zc * cos + pltpu.roll(zc, LANES - ROT_DIM // 2, 1) * sa
                       + pltpu.roll(zc, ROT_DIM // 2, 1) * sb)
                if which == 0:
                    out_ref[:, c + j:c + j + LANES] = (rot * ATTN_SCALE).astype(BF16)
                else:
                    out_ref[:, c + j:c + j + LANES] = rot

    v_w = v_ref.shape[1]
    for c in range(0, v_w, col_chunk):
        base = pool_w + 2 * qk_w + c
        v_ref[:, c:c + col_chunk] = jnp.dot(h, w_ref[:, base:base + col_chunk], preferred_element_type=F32)

    pos = pos0 + t * tm + lax.broadcasted_iota(jnp.int32, (tm, 1), 0)
    for g, w in enumerate(POOL_WINDOWS):
        sl = slice(g * POOL_GROUP_DIM, (g + 1) * POOL_GROUP_DIM)
        u = ext_ref[POOL_HDR:POOL_HDR + tm, sl]
        ws = u
        for j in range(1, w):
            ws = ws + ext_ref[POOL_HDR - j:POOL_HDR - j + tm, sl]
        cnt = jnp.minimum(w, pos + 1).astype(F32)
        d = (ws / cnt - u).astype(BF16)
        y = jnp.dot(d, pw_ref[g], preferred_element_type=F32) * ps_ref[:, sl]
        py_ref[:, sl] = y.astype(BF16)


def _inproj(x, pool_hdr, pos0, norm_g, w_a, pool_w, pool_scale, n_heads):
    B, T, D = x.shape
    P = pool_scale.shape[-1]
    qk_w = n_heads * 2 * HEAD_DIM
    v_w = n_heads * V_DIM
    tm = min(T, 512)
    assert T % tm == 0 and T >= POOL_HDR and tm % 16 == 0
    nt = T // tm

    pos = pos0 + jnp.arange(T)
    inv_freq = jnp.power(ROPE_THETA, -jnp.arange(0, ROT_DIM, 2, dtype=F32) / ROT_DIM)
    ang = pos.astype(F32)[:, None] * inv_freq[None, :]
    cos, sin = jnp.cos(ang), jnp.sin(ang)
    half = ROT_DIM // 2
    ones = jnp.ones((T, HEAD_DIM - ROT_DIM), F32)
    zeros_h = jnp.zeros((T, half), F32)
    zeros_r = jnp.zeros((T, HEAD_DIM - ROT_DIM), F32)
    cos_t = jnp.tile(jnp.concatenate([cos, cos, ones], axis=1), (1, LANES // HEAD_DIM))
    sa_t = jnp.tile(jnp.concatenate([-sin, zeros_h, zeros_r], axis=1), (1, LANES // HEAD_DIM))
    sb_t = jnp.tile(jnp.concatenate([zeros_h, sin, zeros_r], axis=1), (1, LANES // HEAD_DIM))

    row = lambda w: pl.BlockSpec((None, tm, w), lambda t, b: (b, t, 0))
    tab = pl.BlockSpec((tm, LANES), lambda t, b: (t, 0))
    hdr = pl.BlockSpec((None, POOL_HDR, P), lambda t, b: (b, 0, 0))
    hdr_out = pl.BlockSpec((None, POOL_HDR, P), lambda t, b: (jnp.where(t == nt - 1, b, 0), 0, 0))
    kern = functools.partial(_inproj_kernel, pos0=pos0, n_heads=n_heads)
    return pl.pallas_call(
        kern,
        grid=(nt, B),
        in_specs=[row(D), _const_spec((1, D)), _const_spec(w_a.shape), tab, tab, tab, hdr,
                  _const_spec(pool_w.shape), _const_spec((1, P))],
        out_specs=[row(qk_w), row(qk_w), row(v_w), row(P), hdr_out],
        out_shape=[jax.ShapeDtypeStruct((B, T, qk_w), BF16),
                   jax.ShapeDtypeStruct((B, T, qk_w), F32),
                   jax.ShapeDtypeStruct((B, T, v_w), F32),
                   jax.ShapeDtypeStruct((B, T, P), BF16),
                   jax.ShapeDtypeStruct((B, POOL_HDR, P), F32)],
        scratch_shapes=[pltpu.VMEM((POOL_HDR + tm, P), F32), pltpu.VMEM((B, POOL_HDR, P), F32)],
        compiler_params=_params(("arbitrary", "arbitrary")),
        name="inproj",
    )(x, norm_g.reshape(1, D), w_a, cos_t, sa_t, sb_t, pool_hdr, pool_w, pool_scale.reshape(1, P))


def _lambda(lam_ref):
    lv = lam_ref[...]
    a = jnp.sum(lv[0:1] * lv[1:2], axis=-1, keepdims=True)
    c = jnp.sum(lv[2:3] * lv[3:4], axis=-1, keepdims=True)
    return jnp.exp(a) - jnp.exp(c) + LAM_INIT


def _split_maps(q):
    lane = lax.broadcasted_iota(jnp.int32, q.shape, 1)
    zero = jnp.zeros_like(q)
    return jnp.where(lane < HEAD_DIM, q, zero), jnp.where(lane >= HEAD_DIM, q, zero)


def _nt_dot(a, b):
    return lax.dot_general(a, b, (((1,), (1,)), ((), ())), preferred_element_type=F32)


def _head_out(o, g):
    ms = jnp.mean(o * o, axis=-1, keepdims=True)
    return ((o * lax.rsqrt(ms + RMS_EPS)) * g) * (1.0 - LAM_INIT)


def _attn_prompt_kernel(q_ref, k_ref, v_ref, lam_ref, g_ref, o_ref,
                        kb_ref, vb_ref, acc1_ref, acc2_ref):
    i = pl.program_id(2)
    tq = q_ref.shape[0]

    @pl.when(i == 0)
    def _():
        kb_ref[...] = k_ref[...].astype(BF16)
        vb_ref[...] = v_ref[...].astype(BF16)

    q1, q2 = _split_maps(q_ref[...])
    acc1_ref[...] = jnp.zeros_like(acc1_ref)
    acc2_ref[...] = jnp.zeros_like(acc2_ref)

    def tile(j, carry, mask):
        m1, l1, m2, l2 = carry
        start = pl.multiple_of(j * tq, tq)
        kj = kb_ref[pl.ds(start, tq), :]
        vj = vb_ref[pl.ds(start, tq), :]
        new = []
        for qm, m, l, acc_ref in ((q1, m1, l1, acc1_ref), (q2, m2, l2, acc2_ref)):
            s = _nt_dot(qm, kj)
            if mask is not None:
                s = jnp.where(mask, s, NEG_BIG)
            mn = jnp.maximum(m, jnp.max(s, axis=-1, keepdims=True))
            alpha = jnp.exp(m - mn)
            p = jnp.exp(s - mn)
            ln = alpha * l + jnp.sum(p, axis=-1, keepdims=True)
            acc_ref[...] = alpha * acc_ref[...] + jnp.dot(p.astype(BF16), vj, preferred_element_type=F32)
            new += [mn, ln]
        return tuple(new)

    init = (jnp.full((tq, 1), NEG_BIG, F32), jnp.zeros((tq, 1), F32),
            jnp.full((tq, 1), NEG_BIG, F32), jnp.zeros((tq, 1), F32))
    carry = lax.fori_loop(0, i, lambda j, c: tile(j, c, None), init)
    r = lax.broadcasted_iota(jnp.int32, (tq, tq), 0) // CHUNK
    c = lax.broadcasted_iota(jnp.int32, (tq, tq), 1) // CHUNK
    m1, l1, m2, l2 = tile(i, carry, c <= r)

    lam = _lambda(lam_ref)
    o = acc1_ref[...] / l1 - lam * (acc2_ref[...] / l2)
    o_ref[...] = _head_out(o, g_ref[...]).astype(BF16)


def _attn_prompt(q, k, v, lam_vecs, subln_g, n_heads):
    B, T, _ = q.shape
    tq = min(T, 256)
    assert T % tq == 0 and tq % CHUNK == 0
    qblk = pl.BlockSpec((None, tq, V_DIM), lambda b, h, i: (b, i, h))
    kvblk = pl.BlockSpec((None, T, V_DIM), lambda b, h, i: (b, 0, h))
    return pl.pallas_call(
        _attn_prompt_kernel,
        grid=(B, n_heads, T // tq),
        in_specs=[qblk, kvblk, kvblk, _const_spec(lam_vecs.shape), _const_spec((1, V_DIM))],
        out_specs=qblk,
        out_shape=jax.ShapeDtypeStruct((B, T, n_heads * V_DIM), BF16),
        scratch_shapes=[pltpu.VMEM((T, V_DIM), BF16), pltpu.VMEM((T, V_DIM), BF16),
                        pltpu.VMEM((tq, V_DIM), F32), pltpu.VMEM((tq, V_DIM), F32)],
        compiler_params=_params(("arbitrary", "arbitrary", "arbitrary")),
        name="attn_prompt",
    )(q, k, v, lam_vecs, subln_g.reshape(1, V_DIM))


def _attn_sample_kernel(q_ref, ck_ref, cv_ref, nk_ref, nv_ref, lam_ref, g_ref, o_ref, *, past):
    S = q_ref.shape[0]
    P = ck_ref.shape[0]
    q1, q2 = _split_maps(q_ref[...])
    ck = ck_ref[...].astype(BF16)
    nk = nk_ref[...].astype(BF16)
    cv = cv_ref[...].astype(BF16)
    nv = nv_ref[...].astype(BF16)
    q_chunk = (past + lax.broadcasted_iota(jnp.int32, (S, 1), 0)) // CHUNK
    mask_c = (lax.broadcasted_iota(jnp.int32, (S, P), 1) // CHUNK) <= q_chunk
    mask_n = ((past + lax.broadcasted_iota(jnp.int32, (S, S), 1)) // CHUNK) <= q_chunk
    outs = []
    for qm in (q1, q2):
        sc = jnp.where(mask_c, _nt_dot(qm, ck), NEG_BIG)
        sn = jnp.where(mask_n, _nt_dot(qm, nk), NEG_BIG)
        m = jnp.maximum(jnp.max(sc, axis=-1, keepdims=True), jnp.max(sn, axis=-1, keepdims=True))
        pc = jnp.exp(sc - m)
        pn = jnp.exp(sn - m)
        l = jnp.sum(pc, axis=-1, keepdims=True) + jnp.sum(pn, axis=-1, keepdims=True)
        acc = (jnp.dot(pc.astype(BF16), cv, preferred_element_type=F32)
               + jnp.dot(pn.astype(BF16), nv, preferred_element_type=F32))
        outs.append(acc / l)
    o = outs[0] - _lambda(lam_ref) * outs[1]
    o_ref[...] = _head_out(o, g_ref[...]).astype(BF16)


def _attn_sample(q, k, v, cache_k, cache_v, lam_vecs, subln_g, n_heads):
    B, S, _ = q.shape
    P = cache_k.shape[1]
    nblk = pl.BlockSpec((None, S, V_DIM), lambda b, h: (b, 0, h))
    cblk = pl.BlockSpec((None, P, V_DIM), lambda b, h: (b, 0, h))
    kern = functools.partial(_attn_sample_kernel, past=P)
    return pl.pallas_call(
        kern,
        grid=(B, n_heads),
        in_specs=[nblk, cblk, cblk, nblk, nblk, _const_spec(lam_vecs.shape), _const_spec((1, V_DIM))],
        out_specs=nblk,
        out_shape=jax.ShapeDtypeStruct((B, S, n_heads * V_DIM), BF16),
        compiler_params=_params(("arbitrary", "arbitrary")),
        name="attn_sample",
    )(q, cache_k, cache_v, k, v, lam_vecs, subln_g.reshape(1, V_DIM))


def _merge_kernel(x_ref, py_ref, at_ref, gmix_ref, wg_ref, bg_ref, wpb_ref, wab_ref, wo_ref,
                  gffn_ref, wr_ref, br_ref, tri_ref,
                  x1_ref, ri_ref, wc_ref, cnt_ref, wrow_ref, run_ref):
    step = pl.program_id(0)
    tm, D = x_ref.shape

    @pl.when(step == 0)
    def _():
        wrow_ref[...] = jnp.zeros_like(wrow_ref)
        run_ref[...] = jnp.zeros_like(run_ref)

    x = x_ref[...]
    h = _rms(x, gmix_ref[...]).astype(BF16)
    gl = jnp.dot(h, wg_ref[...], preferred_element_type=F32) + bg_ref[...]
    a = jnp.dot(py_ref[...], wpb_ref[...], preferred_element_type=F32)
    bb = jnp.dot(at_ref[...], wab_ref[...], preferred_element_type=F32)
    merged = jax.nn.sigmoid(gl[:, :D]) * a + jax.nn.sigmoid(gl[:, D:]) * bb
    x1 = x + jnp.dot(merged.astype(BF16), wo_ref[...], preferred_element_type=F32)
    x1_ref[...] = x1

    h2 = _rms(x1, gffn_ref[...]).astype(BF16)
    logits = jnp.dot(h2, wr_ref[...], preferred_element_type=F32) + br_ref[...]
    lt = logits.T
    G = N_EXPERT_GROUPS
    E = EXPERTS_PER_GROUP
    best = lt[0:1]
    gsel = jnp.zeros((1, tm), jnp.int32)
    for g in range(1, G):
        better = lt[g:g + 1] > best
        gsel = jnp.where(better, g, gsel)
        best = jnp.where(better, lt[g:g + 1], best)
    gden = jnp.zeros((1, tm), F32)
    for g in range(G):
        gden = gden + jnp.exp(lt[g:g + 1] - best)
    g_w = 1.0 / gden
    el = [lt[G + e:G + e + 1] for e in range(E)]
    for g in range(1, G):
        sel = gsel == g
        el = [jnp.where(sel, lt[G + g * E + e:G + g * E + e + 1], el[e]) for e in range(E)]
    v0 = el[0]
    i0 = jnp.zeros((1, tm), jnp.int32)
    for e in range(1, E):
        better = el[e] > v0
        i0 = jnp.where(better, e, i0)
        v0 = jnp.where(better, el[e], v0)
    v1 = jnp.full((1, tm), -jnp.inf, F32)
    i1 = jnp.zeros((1, tm), jnp.int32)
    for e in range(E):
        better = (i0 != e) & (el[e] > v1)
        i1 = jnp.where(better, e, i1)
        v1 = jnp.where(better, el[e], v1)
    ex = jnp.exp(v1 - v0)
    w0 = (1.0 / (1.0 + ex)) * g_w
    w1 = (ex / (1.0 + ex)) * g_w
    e0 = gsel * E + i0
    e1 = gsel * E + i1

    eid = lax.broadcasted_iota(jnp.int32, (N_EXPERTS, tm), 0)
    oh0 = eid == e0
    oh1 = eid == e1
    oh = jnp.where(oh0 | oh1, 1.0, 0.0).astype(BF16)
    pref = jnp.dot(oh, tri_ref[...], preferred_element_type=F32)
    rank = run_ref[...] + pref - 1.0
    r0 = jnp.sum(jnp.where(oh0, rank, 0.0), axis=0, keepdims=True)
    r1 = jnp.sum(jnp.where(oh1, rank, 0.0), axis=0, keepdims=True)
    run_ref[...] = run_ref[...] + pref[:, tm - 1:tm]
    cnt_ref[...] = jnp.broadcast_to(run_ref[...], cnt_ref.shape).astype(jnp.int32)

    ri_ref[0:1, :] = e0
    ri_ref[1:2, :] = e1
    ri_ref[2:3, :] = r0.astype(jnp.int32)
    ri_ref[3:4, :] = r1.astype(jnp.int32)
    wrow_ref[0:1, :] = w0
    wrow_ref[1:2, :] = w1
    wc_ref[...] = wrow_ref[...].T


def _merge(x, py, at, p):
    N, D = x.shape
    tm = min(N, 512)
    assert N % tm == 0
    tri = (jnp.arange(tm)[:, None] <= jnp.arange(tm)[None, :]).astype(BF16)
    row = lambda w: pl.BlockSpec((tm, w), lambda i: (i, 0))
    consts = [p["gmix"], p["w_gate"], p["b_gate"], p["w_pb"], p["w_ab"], p["w_out"],
              p["gffn"], p["w_router"], p["b_router"], tri]
    return pl.pallas_call(
        _merge_kernel,
        grid=(N // tm,),
        in_specs=[row(D), row(py.shape[1]), row(at.shape[1])] + [_const_spec(c.shape) for c in consts],
        out_specs=[row(D), pl.BlockSpec((4, tm), lambda i: (0, i)), row(LANES),
                   _const_spec((N_EXPERTS, LANES))],
        out_shape=[jax.ShapeDtypeStruct((N, D), F32),
                   jax.ShapeDtypeStruct((4, N), jnp.int32),
                   jax.ShapeDtypeStruct((N, LANES), F32),
                   jax.ShapeDtypeStruct((N_EXPERTS, LANES), jnp.int32)],
        scratch_shapes=[pltpu.VMEM((LANES, tm), F32), pltpu.VMEM((N_EXPERTS, 1), F32)],
        compiler_params=_params(("arbitrary",)),
        name="merge",
    )(x, py, at, *consts)


def _dispatch_kernel(pos_ref, x1_ref, g_ref, xs_ref, buf_ref, sem_ref):
    i = pl.program_id(0)
    n = pl.num_programs(0)
    td = x1_ref.shape[0]
    slot = i % 2

    def drain(s):
        for _ in range(2):
            pltpu.make_async_copy(buf_ref.at[s], xs_ref.at[pl.ds(0, td)], sem_ref.at[s]).wait()

    @pl.when(i >= 2)
    def _():
        drain(slot)

    buf_ref[slot] = _rms(x1_ref[...], g_ref[...])

    def issue(r, _):
        for c in range(2):
            pltpu.make_async_copy(buf_ref.at[slot, pl.ds(r, 1)], xs_ref.at[pl.ds(pos_ref[c, r], 1)],
                                  sem_ref.at[slot]).start()
        return 0

    lax.fori_loop(0, td, issue, 0)

    @pl.when(i == n - 1)
    def _():
        drain(slot)

        @pl.when(n >= 2)
        def _():
            drain(1 - slot)


def _dispatch(x1, pos, gffn):
    N, D = x1.shape
    td = min(N, 512)
    assert N % td == 0
    nt = N // td
    pos3 = pos.reshape(2, nt, td).transpose(1, 0, 2)
    return pl.pallas_call(
        _dispatch_kernel,
        grid=(nt,),
        in_specs=[pl.BlockSpec((None, 2, td), lambda i: (i, 0, 0), memory_space=pltpu.SMEM),
                  pl.BlockSpec((td, D), lambda i: (i, 0)), _const_spec((1, D))],
        out_specs=pl.BlockSpec(memory_space=pl.ANY),
        out_shape=jax.ShapeDtypeStruct((2 * N, D), F32),
        scratch_shapes=[pltpu.VMEM((2, td, D), F32), pltpu.SemaphoreType.DMA((2,))],
        compiler_params=_params(("arbitrary",)),
        name="dispatch",
    )(pos3, x1, gffn)


def _ffn_kernel(tile_ref, ex_ref, lo_ref, hi_ref, first_ref, xs_ref, wg_ref, wu_ref, wd_ref, ys_ref):
    i = pl.program_id(0)
    tf = xs_ref.shape[0]
    lo = lo_ref[i]
    hi = hi_ref[i]

    @pl.when(hi > lo)
    def _():
        xb = xs_ref[...].astype(BF16)
        gate = jnp.dot(xb, wg_ref[...], preferred_element_type=F32)
        up = jnp.dot(xb, wu_ref[...], preferred_element_type=F32)
        hid = (jax.nn.silu(gate) * up).astype(BF16)
        y = jnp.dot(hid, wd_ref[...], preferred_element_type=F32)
        rows = tile_ref[i] * tf + lax.broadcasted_iota(jnp.int32, (tf, 1), 0)
        mine = (rows >= lo) & (rows < hi)

        @pl.when(first_ref[i] == 1)
        def _():
            ys_ref[...] = jnp.where(mine, y, 0.0)

        @pl.when(first_ref[i] == 0)
        def _():
            ys_ref[...] = jnp.where(mine, y, ys_ref[...])


def _ffn(xs, sched, w_gate, w_up, w_down, tf):
    M, D = xs.shape
    Hd = w_gate.shape[2]
    n_items = sched[0].shape[0]
    rows = pl.BlockSpec((tf, D), lambda i, tile, ex, lo, hi, first: (tile[i], 0))
    wspec = lambda a, b_: pl.BlockSpec((None, a, b_), lambda i, tile, ex, lo, hi, first: (ex[i], 0, 0))
    return pl.pallas_call(
        _ffn_kernel,
        grid_spec=pltpu.PrefetchScalarGridSpec(
            num_scalar_prefetch=5,
            grid=(n_items,),
            in_specs=[rows, wspec(D, Hd), wspec(D, Hd), wspec(Hd, D)],
            out_specs=rows,
        ),
        out_shape=jax.ShapeDtypeStruct((M, D), F32),
        compiler_params=_params(("arbitrary",)),
        name="expert_ffn",
    )(*sched, xs, w_gate, w_up, w_down)


def _schedule(counts, M, tf):
    n_tiles = M // tf
    n_items = n_tiles + N_EXPERTS - 1
    ends = jnp.cumsum(counts)
    starts = ends - counts
    first_tile = starts // tf
    last_tile = jnp.maximum(ends - 1, 0) // tf
    per = jnp.where(counts > 0, last_tile - first_tile + 1, 0)
    item_end = jnp.cumsum(per)
    item_start = item_end - per
    total = item_end[-1]
    idx = jnp.arange(n_items, dtype=jnp.int32)
    valid = idx < total
    idx_c = jnp.minimum(idx, total - 1)
    ex = jnp.minimum(jnp.searchsorted(item_end, idx_c, side="right"), N_EXPERTS - 1).astype(jnp.int32)
    tile = (first_tile[ex] + idx_c - item_start[ex]).astype(jnp.int32)
    lo = jnp.where(valid, starts[ex], 0).astype(jnp.int32)
    hi = jnp.where(valid, ends[ex], 0).astype(jnp.int32)
    prev = jnp.concatenate([jnp.full((1,), -1, jnp.int32), tile[:-1]])
    first = (valid & (tile != prev)).astype(jnp.int32)
    return (tile, ex, lo, hi, first), starts


def _combine_kernel(pos_ref, nxt_ref, x1_ref, wc_ref, g_ref, ys_ref, o_ref, buf_ref, sem_ref):
    i = pl.program_id(0)
    n = pl.num_programs(0)
    tc = x1_ref.shape[0]
    slot = i % 2

    def gather(idx_ref, s):
        def issue(r, _):
            for c in range(2):
                pltpu.make_async_copy(ys_ref.at[pl.ds(idx_ref[c, r], 1)], buf_ref.at[s, c, pl.ds(r, 1)],
                                      sem_ref.at[s]).start()
            return 0
        lax.fori_loop(0, tc, issue, 0)

    @pl.when(i == 0)
    def _():
        gather(pos_ref, slot)

    @pl.when(i + 1 < n)
    def _():
        gather(nxt_ref, 1 - slot)

    for c in range(2):
        pltpu.make_async_copy(ys_ref.at[pl.ds(0, tc)], buf_ref.at[slot, c], sem_ref.at[slot]).wait()

    wc = wc_ref[...]
    y = x1_ref[...] + (wc[:, 0:1] * buf_ref[slot, 0] + wc[:, 1:2] * buf_ref[slot, 1])
    o_ref[...] = _rms(y, g_ref[...])


def _combine(x1, pos, wcol, ys, final_g):
    N, D = x1.shape
    tc = min(N, 256)
    assert N % tc == 0
    nt = N // tc
    pos3 = pos.reshape(2, nt, tc).transpose(1, 0, 2)
    smem = lambda f: pl.BlockSpec((None, 2, tc), f, memory_space=pltpu.SMEM)
    return pl.pallas_call(
        _combine_kernel,
        grid=(nt,),
        in_specs=[smem(lambda i: (i, 0, 0)), smem(lambda i: (jnp.minimum(i + 1, nt - 1), 0, 0)),
                  pl.BlockSpec((tc, D), lambda i: (i, 0)), pl.BlockSpec((tc, LANES), lambda i: (i, 0)),
                  _const_spec((1, D)), pl.BlockSpec(memory_space=pl.ANY)],
        out_specs=pl.BlockSpec((tc, D), lambda i: (i, 0)),
        out_shape=jax.ShapeDtypeStruct((N, D), F32),
        scratch_shapes=[pltpu.VMEM((2, 2, tc, D), F32), pltpu.SemaphoreType.DMA((2,))],
        compiler_params=_params(("arbitrary",)),
        name="combine",
    )(pos3, pos3, x1, wcol, final_g, ys)


def _moe_and_norm(x1, ri, wcol, counts, p):
    N, D = x1.shape
    tf = min(2 * N, 512)
    sched, starts = _schedule(counts, 2 * N, tf)
    pos = starts[ri[0:2]] + ri[2:4]
    xs = _dispatch(x1, pos, p["gffn"])
    ys = _ffn(xs, sched, p["w_eg"], p["w_eu"], p["w_ed"], tf)
    return _combine(x1, pos, wcol, ys, p["final_g"])


def _stream(x, pos0, pool_hdr, cache, p, n_heads):
    B, T, D = x.shape
    q, k, v, py, npool = _inproj(x, pool_hdr, pos0, p["gmix"], p["w_a"], p["pool_w"], p["pool_scale"], n_heads)
    if cache is None:
        at = _attn_prompt(q, k, v, p["lam_vecs"], p["subln_g"], n_heads)
    else:
        at = _attn_sample(q, k, v, cache[0], cache[1], p["lam_vecs"], p["subln_g"], n_heads)
    N = B * T
    x1, ri, wcol, cnt = _merge(x.reshape(N, D), py.reshape(N, -1), at.reshape(N, -1), p)
    y = _moe_and_norm(x1, ri, wcol, cnt[:, 0], p)
    return (y.reshape(B, T, D), k.reshape(1, B, T, n_heads, 2 * HEAD_DIM), v.reshape(1, B, T, n_heads, V_DIM),
            npool[None, :, 1:, :])


def kernel(x_prompt, x_sample, cache_k, cache_v, state_pool, norm_mix_g, w_in, b_gate, lambda_q1, lambda_k1,
           lambda_q2, lambda_k2, subln_g, pool_w, pool_scale, w_pool_branch, w_attn_branch, w_out, norm_ffn_g,
           w_router_group, b_router_group, w_router_expert, b_router_expert, w_expert_gate, w_expert_up,
           w_expert_down, final_norm_g):
    assert w_in.shape[0] == 1, "single layer"
    D = x_prompt.shape[-1]
    n_heads = cache_k.shape[3]
    P = pool_scale.shape[-1]
    past = cache_k.shape[2]
    n_a = P + 2 * n_heads * 2 * HEAD_DIM + n_heads * V_DIM

    wr = jnp.concatenate([w_router_group[0], jnp.transpose(w_router_expert[0], (1, 0, 2)).reshape(D, N_EXPERTS)],
                         axis=1)
    br = jnp.concatenate([b_router_group[0], b_router_expert[0].reshape(N_EXPERTS)])
    n_r = wr.shape[1]
    p = dict(
        gmix=norm_mix_g[0].reshape(1, D),
        w_a=w_in[0, :, :n_a].astype(BF16),
        w_gate=w_in[0, :, n_a:].astype(BF16),
        b_gate=b_gate[0].reshape(1, 2 * D),
        lam_vecs=jnp.stack([lambda_q1[0], lambda_k1[0], lambda_q2[0], lambda_k2[0]]),
        subln_g=subln_g[0],
        pool_w=pool_w[0].astype(BF16),
        pool_scale=pool_scale[0],
        w_pb=w_pool_branch[0].astype(BF16),
        w_ab=w_attn_branch[0].astype(BF16),
        w_out=w_out[0].astype(BF16),
        gffn=norm_ffn_g[0].reshape(1, D),
        w_router=jnp.pad(wr, ((0, 0), (0, LANES - n_r))).astype(BF16),
        b_router=jnp.pad(br, (0, LANES - n_r)).reshape(1, LANES),
        w_eg=w_expert_gate[0].astype(BF16),
        w_eu=w_expert_up[0].astype(BF16),
        w_ed=w_expert_down[0].astype(BF16),
        final_g=final_norm_g.reshape(1, D),
    )

    Bp = x_prompt.shape[0]
    Bs = x_sample.shape[0]
    hdr_p = jnp.zeros((Bp, POOL_HDR, P), F32)
    hdr_s = jnp.pad(state_pool[0], ((0, 0), (1, 0), (0, 0)))
    cache = (cache_k[0].reshape(Bs, past, -1), cache_v[0].reshape(Bs, past, -1))

    yp, kp, vp, pp = _stream(x_prompt, 0, hdr_p, None, p, n_heads)
    ys, ks, vs, ps = _stream(x_sample, past, hdr_s, cache, p, n_heads)
    return (yp, ys, kp, vp, pp, ks, vs, ps)
```

```python
import functools
import math

import jax
import jax.numpy as jnp
from jax import lax
from jax.experimental import pallas as pl
from jax.experimental.pallas import tpu as pltpu

CHUNK = 64
HEAD_DIM = 64
V_DIM = 2 * HEAD_DIM
ROT_DIM = HEAD_DIM // 4
ROPE_THETA = 500000.0
ATTN_SCALE = HEAD_DIM ** -0.5
POOL_WINDOWS = (2, 4, 8, 16)
POOL_GROUP_DIM = 128
POOL_BUF = max(POOL_WINDOWS) - 1
POOL_HDR = POOL_BUF + 1
N_EXPERT_GROUPS = 4
EXPERTS_PER_GROUP = 4
N_EXPERTS = N_EXPERT_GROUPS * EXPERTS_PER_GROUP
RMS_EPS = 1e-6
LAM_INIT = 0.8 - 0.6 * math.exp(-0.3 * 0)

LANES = 128
VMEM_LIMIT = 56 * 1024 * 1024
NEG_BIG = -1e30

F32 = jnp.float32
BF16 = jnp.bfloat16


def _params(semantics):
    return pltpu.CompilerParams(dimension_semantics=semantics, vmem_limit_bytes=VMEM_LIMIT)


def _const_spec(shape):
    nd = len(shape)
    return pl.BlockSpec(shape, lambda *_: (0,) * nd)


def _rms(x, g):
    ms = jnp.mean(x * x, axis=-1, keepdims=True)
    return (x * lax.rsqrt(ms + RMS_EPS)) * g


def _inproj_kernel(x_ref, g_ref, w_ref, cos_ref, sa_ref, sb_ref, pbuf_ref, pw_ref, ps_ref,
                   q_ref, k_ref, v_ref, py_ref, npool_ref, ext_ref, carry_ref, *, pos0, n_heads):
    t = pl.program_id(0)
    b = pl.program_id(1)
    tm = x_ref.shape[0]
    pool_w = ext_ref.shape[1]
    qk_w = n_heads * 2 * HEAD_DIM

    h = _rms(x_ref[...], g_ref[...]).astype(BF16)

    @pl.when(t == 0)
    def _():
        ext_ref[0:POOL_HDR, :] = pbuf_ref[...]

    @pl.when(t > 0)
    def _():
        ext_ref[0:POOL_HDR, :] = carry_ref[b]

    ext_ref[POOL_HDR:, :] = jnp.dot(h, w_ref[:, 0:pool_w], preferred_element_type=F32)
    tail = ext_ref[tm:tm + POOL_HDR, :]
    carry_ref[b] = tail
    npool_ref[...] = tail

    cos = cos_ref[...]
    sa = sa_ref[...]
    sb = sb_ref[...]
    col_chunk = 4 * LANES
    for which, out_ref in ((0, q_ref), (1, k_ref)):
        for c in range(0, qk_w, col_chunk):
            base = pool_w + which * qk_w + c
            z = jnp.dot(h, w_ref[:, base:base + col_chunk], preferred_element_type=F32)
            for j in range(0, col_chunk, LANES):
                zc = z[:, j:j + LANES]
                rot = (zc * cos + pltpu.roll(zc, LANES - ROT_DIM // 2, 1) * sa
                       + pltpu.roll(zc, ROT_DIM // 2, 1) * sb)
                if which == 0:
                    out_ref[:, c + j:c + j + LANES] = (rot * ATTN_SCALE).astype(BF16)
                else:
                    out_ref[:, c + j:c + j + LANES] = rot

    v_w = v_ref.shape[1]
    for c in range(0, v_w, col_chunk):
        base = pool_w + 2 * qk_w + c
        v_ref[:, c:c + col_chunk] = jnp.dot(h, w_ref[:, base:base + col_chunk], preferred_element_type=F32)

    pos = pos0 + t * tm + lax.broadcasted_iota(jnp.int32, (tm, 1), 0)
    for g, w in enumerate(POOL_WINDOWS):
        sl = slice(g * POOL_GROUP_DIM, (g + 1) * POOL_GROUP_DIM)
        u = ext_ref[POOL_HDR:POOL_HDR + tm, sl]
        ws = u
        for j in range(1, w):
            ws = ws + ext_ref[POOL_HDR - j:POOL_HDR - j + tm, sl]
        cnt = jnp.minimum(w, pos + 1).astype(F32)
        d = (ws / cnt - u).astype(BF16)
        y = jnp.dot(d, pw_ref[g], preferred_element_type=F32) * ps_ref[:, sl]
        py_ref[:, sl] = y.astype(BF16)


def _inproj(x, pool_hdr, pos0, norm_g, w_a, pool_w, pool_scale, n_heads):
    B, T, D = x.shape
    P = pool_scale.shape[-1]
    qk_w = n_heads * 2 * HEAD_DIM
    v_w = n_heads * V_DIM
    tm = min(T, 512)
    assert T % tm == 0 and T >= POOL_HDR and tm % 16 == 0
    nt = T // tm

    pos = pos0 + jnp.arange(T)
    inv_freq = jnp.power(ROPE_THETA, -jnp.arange(0, ROT_DIM, 2, dtype=F32) / ROT_DIM)
    ang = pos.astype(F32)[:, None] * inv_freq[None, :]
    cos, sin = jnp.cos(ang), jnp.sin(ang)
    half = ROT_DIM // 2
    ones = jnp.ones((T, HEAD_DIM - ROT_DIM), F32)
    zeros_h = jnp.zeros((T, half), F32)
    zeros_r = jnp.zeros((T, HEAD_DIM - ROT_DIM), F32)
    cos_t = jnp.tile(jnp.concatenate([cos, cos, ones], axis=1), (1, LANES // HEAD_DIM))
    sa_t = jnp.tile(jnp.concatenate([-sin, zeros_h, zeros_r], axis=1), (1, LANES // HEAD_DIM))
    sb_t = jnp.tile(jnp.concatenate([zeros_h, sin, zeros_r], axis=1), (1, LANES // HEAD_DIM))

    row = lambda w: pl.BlockSpec((None, tm, w), lambda t, b: (b, t, 0))
    tab = pl.BlockSpec((tm, LANES), lambda t, b: (t, 0))
    hdr = pl.BlockSpec((None, POOL_HDR, P), lambda t, b: (b, 0, 0))
    hdr_out = pl.BlockSpec((None, POOL_HDR, P), lambda t, b: (jnp.where(t == nt - 1, b, 0), 0, 0))
    kern = functools.partial(_inproj_kernel, pos0=pos0, n_heads=n_heads)
    return pl.pallas_call(
        kern,
        grid=(nt, B),
        in_specs=[row(D), _const_spec((1, D)), _const_spec(w_a.shape), tab, tab, tab, hdr,
                  _const_spec(pool_w.shape), _const_spec((1, P))],
        out_specs=[row(qk_w), row(qk_w), row(v_w), row(P), hdr_out],
        out_shape=[jax.ShapeDtypeStruct((B, T, qk_w), BF16),
                   jax.ShapeDtypeStruct((B, T, qk_w), F32),
                   jax.ShapeDtypeStruct((B, T, v_w), F32),
                   jax.ShapeDtypeStruct((B, T, P), BF16),
                   jax.ShapeDtypeStruct((B, POOL_HDR, P), F32)],
        scratch_shapes=[pltpu.VMEM((POOL_HDR + tm, P), F32), pltpu.VMEM((B, POOL_HDR, P), F32)],
        compiler_params=_params(("arbitrary", "arbitrary")),
        name="inproj",
    )(x, norm_g.reshape(1, D), w_a, cos_t, sa_t, sb_t, pool_hdr, pool_w, pool_scale.reshape(1, P))


def _lambda(lam_ref):
    lv = lam_ref[...]
    a = jnp.sum(lv[0:1] * lv[1:2], axis=-1, keepdims=True)
    c = jnp.sum(lv[2:3] * lv[3:4], axis=-1, keepdims=True)
    return jnp.exp(a) - jnp.exp(c) + LAM_INIT


def _split_maps(q):
    lane = lax.broadcasted_iota(jnp.int32, q.shape, 1)
    zero = jnp.zeros_like(q)
    return jnp.where(lane < HEAD_DIM, q, zero), jnp.where(lane >= HEAD_DIM, q, zero)


def _nt_dot(a, b):
    return lax.dot_general(a, b, (((1,), (1,)), ((), ())), preferred_element_type=F32)


def _head_out(o, g):
    ms = jnp.mean(o * o, axis=-1, keepdims=True)
    return ((o * lax.rsqrt(ms + RMS_EPS)) * g) * (1.0 - LAM_INIT)


def _attn_prompt_kernel(q_ref, k_ref, v_ref, lam_ref, g_ref, o_ref,
                        kb_ref, vt_ref, acc1_ref, acc2_ref, stat_ref, sa_ref, sb_ref):
    i = pl.program_id(2)
    tq = q_ref.shape[0]
    n_kv = vt_ref.shape[0]

    @pl.when(i == 0)
    def _():
        kb_ref[...] = k_ref[...].astype(BF16)
        for c in range(n_kv):
            vt_ref[c] = v_ref[c * tq:(c + 1) * tq, :].T.astype(BF16)

    qt = q_ref[...].astype(F32).T.astype(BF16)
    row = lax.broadcasted_iota(jnp.int32, qt.shape, 0)
    zero = jnp.zeros_like(qt)
    qt1 = jnp.where(row < HEAD_DIM, qt, zero)
    qt2 = jnp.where(row >= HEAD_DIM, qt, zero)
    acc1_ref[...] = jnp.zeros_like(acc1_ref)
    acc2_ref[...] = jnp.zeros_like(acc2_ref)
    for c in range(2):
        stat_ref[2 * c:2 * c + 1, :] = jnp.full((1, tq), NEG_BIG, F32)
        stat_ref[2 * c + 1:2 * c + 2, :] = jnp.zeros((1, tq), F32)

    def scores(j, s_ref):
        kj = kb_ref[pl.ds(pl.multiple_of(j * tq, tq), tq), :]
        s_ref[0] = jnp.dot(kj, qt1, preferred_element_type=F32)
        s_ref[1] = jnp.dot(kj, qt2, preferred_element_type=F32)

    def softmax_pv(j, s_ref, diagonal):
        vtj = vt_ref[j]
        if diagonal:
            key_chunk = lax.broadcasted_iota(jnp.int32, (tq, tq), 0) // CHUNK
            qry_chunk = lax.broadcasted_iota(jnp.int32, (tq, tq), 1) // CHUNK
            mask = key_chunk <= qry_chunk
        for c, acc_ref in ((0, acc1_ref), (1, acc2_ref)):
            s = s_ref[c]
            if diagonal:
                s = jnp.where(mask, s, NEG_BIG)
            m = stat_ref[2 * c:2 * c + 1, :]
            l = stat_ref[2 * c + 1:2 * c + 2, :]
            mn = jnp.maximum(m, jnp.max(s, axis=0, keepdims=True))
            alpha = jnp.exp(m - mn)
            p = jnp.exp(s - mn)
            stat_ref[2 * c:2 * c + 1, :] = mn
            stat_ref[2 * c + 1:2 * c + 2, :] = alpha * l + jnp.sum(p, axis=0, keepdims=True)
            acc_ref[...] = alpha * acc_ref[...] + jnp.dot(vtj, p.astype(BF16), preferred_element_type=F32)

    scores(0, sa_ref)

    def pair(p, _):
        t = 2 * p
        scores(t + 1, sb_ref)
        softmax_pv(t, sa_ref, False)
        scores(t + 2, sa_ref)
        softmax_pv(t + 1, sb_ref, False)
        return 0

    lax.fori_loop(0, i // 2, pair, 0)

    @pl.when(i % 2 == 0)
    def _():
        softmax_pv(i, sa_ref, True)

    @pl.when(i % 2 == 1)
    def _():
        scores(i, sb_ref)
        softmax_pv(i - 1, sa_ref, False)
        softmax_pv(i, sb_ref, True)

    lam = _lambda(lam_ref)
    l1 = stat_ref[1:2, :]
    l2 = stat_ref[3:4, :]
    o = acc1_ref[...] / l1 - lam * (acc2_ref[...] / l2)
    ms = jnp.mean(o * o, axis=0, keepdims=True)
    o = ((o * lax.rsqrt(ms + RMS_EPS)) * g_ref[...]) * (1.0 - LAM_INIT)
    o_ref[...] = o.T.astype(BF16)


def _attn_prompt(q, k, v, lam_vecs, subln_g, n_heads):
    B, T, _ = q.shape
    tq = min(T, 512)
    assert T % tq == 0 and tq % LANES == 0
    qblk = pl.BlockSpec((None, tq, V_DIM), lambda b, h, i: (b, i, h))
    kvblk = pl.BlockSpec((None, T, V_DIM), lambda b, h, i: (b, 0, h))
    return pl.pallas_call(
        _attn_prompt_kernel,
        grid=(B, n_heads, T // tq),
        in_specs=[qblk, kvblk, kvblk, _const_spec(lam_vecs.shape), _const_spec((V_DIM, 1))],
        out_specs=qblk,
        out_shape=jax.ShapeDtypeStruct((B, T, n_heads * V_DIM), BF16),
        scratch_shapes=[pltpu.VMEM((T, V_DIM), BF16), pltpu.VMEM((T // tq, V_DIM, tq), BF16),
                        pltpu.VMEM((V_DIM, tq), F32), pltpu.VMEM((V_DIM, tq), F32),
                        pltpu.VMEM((4, tq), F32),
                        pltpu.VMEM((2, tq, tq), F32), pltpu.VMEM((2, tq, tq), F32)],
        compiler_params=_params(("arbitrary", "arbitrary", "arbitrary")),
        name="attn_prompt",
    )(q, k, v, lam_vecs, subln_g.reshape(V_DIM, 1))


def _attn_sample_kernel(q_ref, ck_ref, cv_ref, nk_ref, nv_ref, lam_ref, g_ref, o_ref, *, past):
    S = q_ref.shape[0]
    P = ck_ref.shape[0]
    q1, q2 = _split_maps(q_ref[...])
    ck = ck_ref[...].astype(BF16)
    nk = nk_ref[...].astype(BF16)
    cv = cv_ref[...].astype(BF16)
    nv = nv_ref[...].astype(BF16)
    q_chunk = (past + lax.broadcasted_iota(jnp.int32, (S, 1), 0)) // CHUNK
    mask_c = (lax.broadcasted_iota(jnp.int32, (S, P), 1) // CHUNK) <= q_chunk
    mask_n = ((past + lax.broadcasted_iota(jnp.int32, (S, S), 1)) // CHUNK) <= q_chunk
    outs = []
    for qm in (q1, q2):
        sc = jnp.where(mask_c, _nt_dot(qm, ck), NEG_BIG)
        sn = jnp.where(mask_n, _nt_dot(qm, nk), NEG_BIG)
        m = jnp.maximum(jnp.max(sc, axis=-1, keepdims=True), jnp.max(sn, axis=-1, keepdims=True))
        pc = jnp.exp(sc - m)
        pn = jnp.exp(sn - m)
        l = jnp.sum(pc, axis=-1, keepdims=True) + jnp.sum(pn, axis=-1, keepdims=True)
        acc = (jnp.dot(pc.astype(BF16), cv, preferred_element_type=F32)
               + jnp.dot(pn.astype(BF16), nv, preferred_element_type=F32))
        outs.append(acc / l)
    o = outs[0] - _lambda(lam_ref) * outs[1]
    o_ref[...] = _head_out(o, g_ref[...]).astype(BF16)


def _attn_sample(q, k, v, cache_k, cache_v, lam_vecs, subln_g, n_heads):
    B, S, _ = q.shape
    P = cache_k.shape[1]
    nblk = pl.BlockSpec((None, S, V_DIM), lambda b, h: (b, 0, h))
    cblk = pl.BlockSpec((None, P, V_DIM), lambda b, h: (b, 0, h))
    kern = functools.partial(_attn_sample_kernel, past=P)
    return pl.pallas_call(
        kern,
        grid=(B, n_heads),
        in_specs=[nblk, cblk, cblk, nblk, nblk, _const_spec(lam_vecs.shape), _const_spec((1, V_DIM))],
        out_specs=nblk,
        out_shape=jax.ShapeDtypeStruct((B, S, n_heads * V_DIM), BF16),
        compiler_params=_params(("arbitrary", "arbitrary")),
        name="attn_sample",
    )(q, cache_k, cache_v, k, v, lam_vecs, subln_g.reshape(1, V_DIM))


def _merge_kernel(x_ref, py_ref, at_ref, gmix_ref, wg_ref, bg_ref, wpb_ref, wab_ref, wo_ref,
                  gffn_ref, wr_ref, br_ref, tri_ref,
                  x1_ref, ri_ref, wc_ref, cnt_ref, wrow_ref, run_ref):
    step = pl.program_id(0)
    tm, D = x_ref.shape

    @pl.when(step == 0)
    def _():
        wrow_ref[...] = jnp.zeros_like(wrow_ref)
        run_ref[...] = jnp.zeros_like(run_ref)

    x = x_ref[...]
    h = _rms(x, gmix_ref[...]).astype(BF16)
    gl = jnp.dot(h, wg_ref[...], preferred_element_type=F32) + bg_ref[...]
    a = jnp.dot(py_ref[...], wpb_ref[...], preferred_element_type=F32)
    bb = jnp.dot(at_ref[...], wab_ref[...], preferred_element_type=F32)
    merged = jax.nn.sigmoid(gl[:, :D]) * a + jax.nn.sigmoid(gl[:, D:]) * bb
    x1 = x + jnp.dot(merged.astype(BF16), wo_ref[...], preferred_element_type=F32)
    x1_ref[...] = x1

    h2 = _rms(x1, gffn_ref[...]).astype(BF16)
    logits = jnp.dot(h2, wr_ref[...], preferred_element_type=F32) + br_ref[...]
    lt = logits.T
    G = N_EXPERT_GROUPS
    E = EXPERTS_PER_GROUP
    best = lt[0:1]
    gsel = jnp.zeros((1, tm), jnp.int32)
    for g in range(1, G):
        better = lt[g:g + 1] > best
        gsel = jnp.where(better, g, gsel)
        best = jnp.where(better, lt[g:g + 1], best)
    gden = jnp.zeros((1, tm), F32)
    for g in range(G):
        gden = gden + jnp.exp(lt[g:g + 1] - best)
    g_w = 1.0 / gden
    el = [lt[G + e:G + e + 1] for e in range(E)]
    for g in range(1, G):
        sel = gsel == g
        el = [jnp.where(sel, lt[G + g * E + e:G + g * E + e + 1], el[e]) for e in range(E)]
    v0 = el[0]
    i0 = jnp.zeros((1, tm), jnp.int32)
    for e in range(1, E):
        better = el[e] > v0
        i0 = jnp.where(better, e, i0)
        v0 = jnp.where(better, el[e], v0)
    v1 = jnp.full((1, tm), -jnp.inf, F32)
    i1 = jnp.zeros((1, tm), jnp.int32)
    for e in range(E):
        better = (i0 != e) & (el[e] > v1)
        i1 = jnp.where(better, e, i1)
        v1 = jnp.where(better, el[e], v1)
    ex = jnp.exp(v1 - v0)
    w0 = (1.0 / (1.0 + ex)) * g_w
    w1 = (ex / (1.0 + ex)) * g_w
    e0 = gsel * E + i0
    e1 = gsel * E + i1

    eid = lax.broadcasted_iota(jnp.int32, (N_EXPERTS, tm), 0)
    oh0 = eid == e0
    oh1 = eid == e1
    oh = jnp.where(oh0 | oh1, 1.0, 0.0).astype(BF16)
    pref = jnp.dot(oh, tri_ref[...], preferred_element_type=F32)
    rank = run_ref[...] + pref - 1.0
    r0 = jnp.sum(jnp.where(oh0, rank, 0.0), axis=0, keepdims=True)
    r1 = jnp.sum(jnp.where(oh1, rank, 0.0), axis=0, keepdims=True)
    run_ref[...] = run_ref[...] + pref[:, tm - 1:tm]
    cnt_ref[...] = jnp.broadcast_to(run_ref[...], cnt_ref.shape).astype(jnp.int32)

    ri_ref[0:1, :] = e0
    ri_ref[1:2, :] = e1
    ri_ref[2:3, :] = r0.astype(jnp.int32)
    ri_ref[3:4, :] = r1.astype(jnp.int32)
    wrow_ref[0:1, :] = w0
    wrow_ref[1:2, :] = w1
    wc_ref[...] = wrow_ref[...].T


def _merge(x, py, at, p):
    N, D = x.shape
    tm = min(N, 512)
    assert N % tm == 0
    tri = (jnp.arange(tm)[:, None] <= jnp.arange(tm)[None, :]).astype(BF16)
    row = lambda w: pl.BlockSpec((tm, w), lambda i: (i, 0))
    consts = [p["gmix"], p["w_gate"], p["b_gate"], p["w_pb"], p["w_ab"], p["w_out"],
              p["gffn"], p["w_router"], p["b_router"], tri]
    return pl.pallas_call(
        _merge_kernel,
        grid=(N // tm,),
        in_specs=[row(D), row(py.shape[1]), row(at.shape[1])] + [_const_spec(c.shape) for c in consts],
        out_specs=[row(D), pl.BlockSpec((4, tm), lambda i: (0, i)), row(LANES),
                   _const_spec((N_EXPERTS, LANES))],
        out_shape=[jax.ShapeDtypeStruct((N, D), F32),
                   jax.ShapeDtypeStruct((4, N), jnp.int32),
                   jax.ShapeDtypeStruct((N, LANES), F32),
                   jax.ShapeDtypeStruct((N_EXPERTS, LANES), jnp.int32)],
        scratch_shapes=[pltpu.VMEM((LANES, tm), F32), pltpu.VMEM((N_EXPERTS, 1), F32)],
        compiler_params=_params(("arbitrary",)),
        name="merge",
    )(x, py, at, *consts)


def _dispatch_kernel(pos_ref, x1_ref, g_ref, xs_ref, buf_ref, sem_ref):
    i = pl.program_id(0)
    n = pl.num_programs(0)
    td = x1_ref.shape[0]
    slot = i % 2

    def drain(s):
        for _ in range(2):
            pltpu.make_async_copy(buf_ref.at[s], xs_ref.at[pl.ds(0, td)], sem_ref.at[s]).wait()

    @pl.when(i >= 2)
    def _():
        drain(slot)

    buf_ref[slot] = _rms(x1_ref[...], g_ref[...])

    def issue(r, _):
        for c in range(2):
            pltpu.make_async_copy(buf_ref.at[slot, pl.ds(r, 1)], xs_ref.at[pl.ds(pos_ref[c, r], 1)],
                                  sem_ref.at[slot]).start()
        return 0

    lax.fori_loop(0, td, issue, 0)

    @pl.when(i == n - 1)
    def _():
        drain(slot)

        @pl.when(n >= 2)
        def _():
            drain(1 - slot)


def _dispatch(x1, pos, gffn):
    N, D = x1.shape
    td = min(N, 512)
    assert N % td == 0
    nt = N // td
    pos3 = pos.reshape(2, nt, td).transpose(1, 0, 2)
    return pl.pallas_call(
        _dispatch_kernel,
        grid=(nt,),
        in_specs=[pl.BlockSpec((None, 2, td), lambda i: (i, 0, 0), memory_space=pltpu.SMEM),
                  pl.BlockSpec((td, D), lambda i: (i, 0)), _const_spec((1, D))],
        out_specs=pl.BlockSpec(memory_space=pl.ANY),
        out_shape=jax.ShapeDtypeStruct((2 * N, D), F32),
        scratch_shapes=[pltpu.VMEM((2, td, D), F32), pltpu.SemaphoreType.DMA((2,))],
        compiler_params=_params(("arbitrary",)),
        name="dispatch",
    )(pos3, x1, gffn)


def _ffn_kernel(tile_ref, ex_ref, lo_ref, hi_ref, first_ref, xs_ref, wg_ref, wu_ref, wd_ref, ys_ref):
    i = pl.program_id(0)
    tf = xs_ref.shape[0]
    lo = lo_ref[i]
    hi = hi_ref[i]

    @pl.when(hi > lo)
    def _():
        xb = xs_ref[...].astype(BF16)
        gate = jnp.dot(xb, wg_ref[...], preferred_element_type=F32)
        up = jnp.dot(xb, wu_ref[...], preferred_element_type=F32)
        hid = (jax.nn.silu(gate) * up).astype(BF16)
        y = jnp.dot(hid, wd_ref[...], preferred_element_type=F32)
        rows = tile_ref[i] * tf + lax.broadcasted_iota(jnp.int32, (tf, 1), 0)
        mine = (rows >= lo) & (rows < hi)

        @pl.when(first_ref[i] == 1)
        def _():
            ys_ref[...] = jnp.where(mine, y, 0.0)

        @pl.when(first_ref[i] == 0)
        def _():
            ys_ref[...] = jnp.where(mine, y, ys_ref[...])


def _ffn(xs, sched, w_gate, w_up, w_down, tf):
    M, D = xs.shape
    Hd = w_gate.shape[2]
    n_items = sched[0].shape[0]
    rows = pl.BlockSpec((tf, D), lambda i, tile, ex, lo, hi, first: (tile[i], 0))
    wspec = lambda a, b_: pl.BlockSpec((None, a, b_), lambda i, tile, ex, lo, hi, first: (ex[i], 0, 0))
    return pl.pallas_call(
        _ffn_kernel,
        grid_spec=pltpu.PrefetchScalarGridSpec(
            num_scalar_prefetch=5,
            grid=(n_items,),
            in_specs=[rows, wspec(D, Hd), wspec(D, Hd), wspec(Hd, D)],
            out_specs=rows,
        ),
        out_shape=jax.ShapeDtypeStruct((M, D), F32),
        compiler_params=_params(("arbitrary",)),
        name="expert_ffn",
    )(*sched, xs, w_gate, w_up, w_down)


def _schedule(counts, M, tf):
    n_tiles = M // tf
    n_items = n_tiles + N_EXPERTS - 1
    ends = jnp.cumsum(counts)
    starts = ends - counts
    first_tile = starts // tf
    last_tile = jnp.maximum(ends - 1, 0) // tf
    per = jnp.where(counts > 0, last_tile - first_tile + 1, 0)
    item_end = jnp.cumsum(per)
    item_start = item_end - per
    total = item_end[-1]
    idx = jnp.arange(n_items, dtype=jnp.int32)
    valid = idx < total
    idx_c = jnp.minimum(idx, total - 1)
    ex = jnp.minimum(jnp.sum(item_end[None, :] <= idx_c[:, None], axis=1), N_EXPERTS - 1).astype(jnp.int32)
    onehot = ex[:, None] == jnp.arange(N_EXPERTS)[None, :]
    pick = lambda table: jnp.sum(jnp.where(onehot, table[None, :], 0), axis=1)
    tile = (pick(first_tile) + idx_c - pick(item_start)).astype(jnp.int32)
    lo = jnp.where(valid, pick(starts), 0).astype(jnp.int32)
    hi = jnp.where(valid, pick(ends), 0).astype(jnp.int32)
    prev = jnp.concatenate([jnp.full((1,), -1, jnp.int32), tile[:-1]])
    first = (valid & (tile != prev)).astype(jnp.int32)
    return (tile, ex, lo, hi, first), starts


def _combine_kernel(pos_ref, nxt_ref, x1_ref, wc_ref, g_ref, ys_ref, o_ref, buf_ref, sem_ref):
    i = pl.program_id(0)
    n = pl.num_programs(0)
    tc = x1_ref.shape[0]
    slot = i % 2

    def gather(idx_ref, s):
        def issue(r, _):
            for c in range(2):
                pltpu.make_async_copy(ys_ref.at[pl.ds(idx_ref[c, r], 1)], buf_ref.at[s, c, pl.ds(r, 1)],
                                      sem_ref.at[s]).start()
            return 0
        lax.fori_loop(0, tc, issue, 0)

    @pl.when(i == 0)
    def _():
        gather(pos_ref, slot)

    @pl.when(i + 1 < n)
    def _():
        gather(nxt_ref, 1 - slot)

    for c in range(2):
        pltpu.make_async_copy(ys_ref.at[pl.ds(0, tc)], buf_ref.at[slot, c], sem_ref.at[slot]).wait()

    wc = wc_ref[...]
    y = x1_ref[...] + (wc[:, 0:1] * buf_ref[slot, 0] + wc[:, 1:2] * buf_ref[slot, 1])
    o_ref[...] = _rms(y, g_ref[...])


def _combine(x1, pos, wcol, ys, final_g):
    N, D = x1.shape
    tc = min(N, 256)
    assert N % tc == 0
    nt = N // tc
    pos3 = pos.reshape(2, nt, tc).transpose(1, 0, 2)
    smem = lambda f: pl.BlockSpec((None, 2, tc), f, memory_space=pltpu.SMEM)
    return pl.pallas_call(
        _combine_kernel,
        grid=(nt,),
        in_specs=[smem(lambda i: (i, 0, 0)), smem(lambda i: (jnp.minimum(i + 1, nt - 1), 0, 0)),
                  pl.BlockSpec((tc, D), lambda i: (i, 0)), pl.BlockSpec((tc, LANES), lambda i: (i, 0)),
                  _const_spec((1, D)), pl.BlockSpec(memory_space=pl.ANY)],
        out_specs=pl.BlockSpec((tc, D), lambda i: (i, 0)),
        out_shape=jax.ShapeDtypeStruct((N, D), F32),
        scratch_shapes=[pltpu.VMEM((2, 2, tc, D), F32), pltpu.SemaphoreType.DMA((2,))],
        compiler_params=_params(("arbitrary",)),
        name="combine",
    )(pos3, pos3, x1, wcol, final_g, ys)


def _moe_and_norm(x1, ri, wcol, counts, p):
    N, D = x1.shape
    tf = min(2 * N, 512)
    sched, starts = _schedule(counts, 2 * N, tf)
    hit = ri[0:2][:, None, :] == jnp.arange(N_EXPERTS, dtype=jnp.int32)[None, :, None]
    pos = jnp.sum(jnp.where(hit, starts.astype(jnp.int32)[None, :, None], 0), axis=1) + ri[2:4]
    xs = _dispatch(x1, pos, p["gffn"])
    ys = _ffn(xs, sched, p["w_eg"], p["w_eu"], p["w_ed"], tf)
    return _combine(x1, pos, wcol, ys, p["final_g"])


def _stream(x, pos0, pool_hdr, cache, p, n_heads):
    B, T, D = x.shape
    q, k, v, py, npool = _inproj(x, pool_hdr, pos0, p["gmix"], p["w_a"], p["pool_w"], p["pool_scale"], n_heads)
    if cache is None:
        at = _attn_prompt(q, k, v, p["lam_vecs"], p["subln_g"], n_heads)
    else:
        at = _attn_sample(q, k, v, cache[0], cache[1], p["lam_vecs"], p["subln_g"], n_heads)
    N = B * T
    x1, ri, wcol, cnt = _merge(x.reshape(N, D), py.reshape(N, -1), at.reshape(N, -1), p)
    y = _moe_and_norm(x1, ri, wcol, cnt[:, 0], p)
    return (y.reshape(B, T, D), k.reshape(1, B, T, n_heads, 2 * HEAD_DIM), v.reshape(1, B, T, n_heads, V_DIM),
            npool[None, :, 1:, :])


def kernel(x_prompt, x_sample, cache_k, cache_v, state_pool, norm_mix_g, w_in, b_gate, lambda_q1, lambda_k1,
           lambda_q2, lambda_k2, subln_g, pool_w, pool_scale, w_pool_branch, w_attn_branch, w_out, norm_ffn_g,
           w_router_group, b_router_group, w_router_expert, b_router_expert, w_expert_gate, w_expert_up,
           w_expert_down, final_norm_g):
    assert w_in.shape[0] == 1, "single layer"
    D = x_prompt.shape[-1]
    n_heads = cache_k.shape[3]
    P = pool_scale.shape[-1]
    past = cache_k.shape[2]
    n_a = P + 2 * n_heads * 2 * HEAD_DIM + n_heads * V_DIM

    wr = jnp.concatenate([w_router_group[0], jnp.transpose(w_router_expert[0], (1, 0, 2)).reshape(D, N_EXPERTS)],
                         axis=1)
    br = jnp.concatenate([b_router_group[0], b_router_expert[0].reshape(N_EXPERTS)])
    n_r = wr.shape[1]
    p = dict(
        gmix=norm_mix_g[0].reshape(1, D),
        w_a=w_in[0, :, :n_a].astype(BF16),
        w_gate=w_in[0, :, n_a:].astype(BF16),
        b_gate=b_gate[0].reshape(1, 2 * D),
        lam_vecs=jnp.stack([lambda_q1[0], lambda_k1[0], lambda_q2[0], lambda_k2[0]]),
        subln_g=subln_g[0],
        pool_w=pool_w[0].astype(BF16),
        pool_scale=pool_scale[0],
        w_pb=w_pool_branch[0].astype(BF16),
        w_ab=w_attn_branch[0].astype(BF16),
        w_out=w_out[0].astype(BF16),
        gffn=norm_ffn_g[0].reshape(1, D),
        w_router=jnp.pad(wr, ((0, 0), (0, LANES - n_r))).astype(BF16),
        b_router=jnp.pad(br, (0, LANES - n_r)).reshape(1, LANES),
        w_eg=w_expert_gate[0].astype(BF16),
        w_eu=w_expert_up[0].astype(BF16),
        w_ed=w_expert_down[0].astype(BF16),
        final_g=final_norm_g.reshape(1, D),
    )

    Bp = x_prompt.shape[0]
    Bs = x_sample.shape[0]
    hdr_p = jnp.zeros((Bp, POOL_HDR, P), F32)
    hdr_s = jnp.pad(state_pool[0], ((0, 0), (1, 0), (0, 0)))
    cache = (cache_k[0].reshape(Bs, past, -1), cache_v[0].reshape(Bs, past, -1))

    yp, kp, vp, pp = _stream(x_prompt, 0, hdr_p, None, p, n_heads)
    ys, ks, vs, ps = _stream(x_sample, past, hdr_s, cache, p, n_heads)
    return (yp, ys, kp, vp, pp, ks, vs, ps)
```

```python
import functools
import math

import jax
import jax.numpy as jnp
from jax import lax
from jax.experimental import pallas as pl
from jax.experimental.pallas import tpu as pltpu

CHUNK = 64
HEAD_DIM = 64
V_DIM = 2 * HEAD_DIM
ROT_DIM = HEAD_DIM // 4
ROPE_THETA = 500000.0
ATTN_SCALE = HEAD_DIM ** -0.5
POOL_WINDOWS = (2, 4, 8, 16)
POOL_GROUP_DIM = 128
POOL_BUF = max(POOL_WINDOWS) - 1
POOL_HDR = POOL_BUF + 1
N_EXPERT_GROUPS = 4
EXPERTS_PER_GROUP = 4
N_EXPERTS = N_EXPERT_GROUPS * EXPERTS_PER_GROUP
RMS_EPS = 1e-6
LAM_INIT = 0.8 - 0.6 * math.exp(-0.3 * 0)

LANES = 128
ISSUE_UNROLL = 8
ONES_ROWS = 16
VMEM_LIMIT = 56 * 1024 * 1024
NEG_BIG = -1e30

F32 = jnp.float32
BF16 = jnp.bfloat16


def _params(semantics):
    return pltpu.CompilerParams(dimension_semantics=semantics, vmem_limit_bytes=VMEM_LIMIT)


def _const_spec(shape):
    nd = len(shape)
    return pl.BlockSpec(shape, lambda *_: (0,) * nd)


def _rms(x, g):
    ms = jnp.mean(x * x, axis=-1, keepdims=True)
    return (x * lax.rsqrt(ms + RMS_EPS)) * g


def _inproj_kernel(x_ref, g_ref, w_ref, cos_ref, sa_ref, sb_ref, pbuf_ref, pw_ref, ps_ref,
                   q_ref, k_ref, v_ref, py_ref, npool_ref, ext_ref, carry_ref, *, pos0, n_heads):
    t = pl.program_id(0)
    b = pl.program_id(1)
    tm = x_ref.shape[0]
    pool_w = ext_ref.shape[1]
    qk_w = n_heads * 2 * HEAD_DIM

    h = _rms(x_ref[...], g_ref[...]).astype(BF16)

    @pl.when(t == 0)
    def _():
        ext_ref[0:POOL_HDR, :] = pbuf_ref[...]

    @pl.when(t > 0)
    def _():
        ext_ref[0:POOL_HDR, :] = carry_ref[b]

    ext_ref[POOL_HDR:, :] = jnp.dot(h, w_ref[:, 0:pool_w], preferred_element_type=F32)
    tail = ext_ref[tm:tm + POOL_HDR, :]
    carry_ref[b] = tail
    npool_ref[...] = tail

    cos = cos_ref[...]
    sa = sa_ref[...]
    sb = sb_ref[...]
    col_chunk = 4 * LANES
    for which, out_ref in ((0, q_ref), (1, k_ref)):
        for c in range(0, qk_w, col_chunk):
            base = pool_w + which * qk_w + c
            z = jnp.dot(h, w_ref[:, base:base + col_chunk], preferred_element_type=F32)
            for j in range(0, col_chunk, LANES):
                zc = z[:, j:j + LANES]
                rot = (zc * cos + pltpu.roll(zc, LANES - ROT_DIM // 2, 1) * sa
                       + pltpu.roll(zc, ROT_DIM // 2, 1) * sb)
                if which == 0:
                    out_ref[:, c + j:c + j + LANES] = (rot * ATTN_SCALE).astype(BF16)
                else:
                    out_ref[:, c + j:c + j + LANES] = rot

    v_w = v_ref.shape[1]
    for c in range(0, v_w, col_chunk):
        base = pool_w + 2 * qk_w + c
        v_ref[:, c:c + col_chunk] = jnp.dot(h, w_ref[:, base:base + col_chunk], preferred_element_type=F32)

    pos = pos0 + t * tm + lax.broadcasted_iota(jnp.int32, (tm, 1), 0)
    for g, w in enumerate(POOL_WINDOWS):
        sl = slice(g * POOL_GROUP_DIM, (g + 1) * POOL_GROUP_DIM)
        u = ext_ref[POOL_HDR:POOL_HDR + tm, sl]
        ws = u
        for j in range(1, w):
            ws = ws + ext_ref[POOL_HDR - j:POOL_HDR - j + tm, sl]
        cnt = jnp.minimum(w, pos + 1).astype(F32)
        d = (ws / cnt - u).astype(BF16)
        y = jnp.dot(d, pw_ref[g], preferred_element_type=F32) * ps_ref[:, sl]
        py_ref[:, sl] = y.astype(BF16)


def _inproj(x, pool_hdr, pos0, norm_g, w_a, pool_w, pool_scale, n_heads):
    B, T, D = x.shape
    P = pool_scale.shape[-1]
    qk_w = n_heads * 2 * HEAD_DIM
    v_w = n_heads * V_DIM
    tm = min(T, 512)
    assert T % tm == 0 and T >= POOL_HDR and tm % 16 == 0
    nt = T // tm

    pos = pos0 + jnp.arange(T)
    inv_freq = jnp.power(ROPE_THETA, -jnp.arange(0, ROT_DIM, 2, dtype=F32) / ROT_DIM)
    ang = pos.astype(F32)[:, None] * inv_freq[None, :]
    cos, sin = jnp.cos(ang), jnp.sin(ang)
    half = ROT_DIM // 2
    ones = jnp.ones((T, HEAD_DIM - ROT_DIM), F32)
    zeros_h = jnp.zeros((T, half), F32)
    zeros_r = jnp.zeros((T, HEAD_DIM - ROT_DIM), F32)
    cos_t = jnp.tile(jnp.concatenate([cos, cos, ones], axis=1), (1, LANES // HEAD_DIM))
    sa_t = jnp.tile(jnp.concatenate([-sin, zeros_h, zeros_r], axis=1), (1, LANES // HEAD_DIM))
    sb_t = jnp.tile(jnp.concatenate([zeros_h, sin, zeros_r], axis=1), (1, LANES // HEAD_DIM))

    row = lambda w: pl.BlockSpec((None, tm, w), lambda t, b: (b, t, 0))
    tab = pl.BlockSpec((tm, LANES), lambda t, b: (t, 0))
    hdr = pl.BlockSpec((None, POOL_HDR, P), lambda t, b: (b, 0, 0))
    hdr_out = pl.BlockSpec((None, POOL_HDR, P), lambda t, b: (jnp.where(t == nt - 1, b, 0), 0, 0))
    kern = functools.partial(_inproj_kernel, pos0=pos0, n_heads=n_heads)
    return pl.pallas_call(
        kern,
        grid=(nt, B),
        in_specs=[row(D), _const_spec((1, D)), _const_spec(w_a.shape), tab, tab, tab, hdr,
                  _const_spec(pool_w.shape), _const_spec((1, P))],
        out_specs=[row(qk_w), row(qk_w), row(v_w), row(P), hdr_out],
        out_shape=[jax.ShapeDtypeStruct((B, T, qk_w), BF16),
                   jax.ShapeDtypeStruct((B, T, qk_w), F32),
                   jax.ShapeDtypeStruct((B, T, v_w), F32),
                   jax.ShapeDtypeStruct((B, T, P), BF16),
                   jax.ShapeDtypeStruct((B, POOL_HDR, P), F32)],
        scratch_shapes=[pltpu.VMEM((POOL_HDR + tm, P), F32), pltpu.VMEM((B, POOL_HDR, P), F32)],
        compiler_params=_params(("arbitrary", "arbitrary")),
        name="inproj",
    )(x, norm_g.reshape(1, D), w_a, cos_t, sa_t, sb_t, pool_hdr, pool_w, pool_scale.reshape(1, P))


def _lambda(lam_ref):
    lv = lam_ref[...]
    a = jnp.sum(lv[0:1] * lv[1:2], axis=-1, keepdims=True)
    c = jnp.sum(lv[2:3] * lv[3:4], axis=-1, keepdims=True)
    return jnp.exp(a) - jnp.exp(c) + LAM_INIT


def _split_maps(q):
    lane = lax.broadcasted_iota(jnp.int32, q.shape, 1)
    zero = jnp.zeros_like(q)
    return jnp.where(lane < HEAD_DIM, q, zero), jnp.where(lane >= HEAD_DIM, q, zero)


def _nt_dot(a, b):
    return lax.dot_general(a, b, (((1,), (1,)), ((), ())), preferred_element_type=F32)


def _head_out(o, g):
    ms = jnp.mean(o * o, axis=-1, keepdims=True)
    return ((o * lax.rsqrt(ms + RMS_EPS)) * g) * (1.0 - LAM_INIT)


def _attn_prompt_kernel(q_ref, k_ref, v_ref, lam_ref, g_ref, o_ref,
                        kb_ref, vt_ref, acc1_ref, acc2_ref, stat_ref, sa_ref, sb_ref):
    i = pl.program_id(2)
    tq = q_ref.shape[0]
    n_kv = vt_ref.shape[0]

    @pl.when(i == 0)
    def _():
        kb_ref[...] = k_ref[...].astype(BF16)
        ones_row = jnp.where(lax.broadcasted_iota(jnp.int32, (ONES_ROWS, tq), 0) == 0, 1.0, 0.0).astype(BF16)
        for c in range(n_kv):
            vt_ref[c, 0:V_DIM, :] = v_ref[c * tq:(c + 1) * tq, :].T.astype(BF16)
            vt_ref[c, V_DIM:, :] = ones_row

    qt = q_ref[...].astype(F32).T.astype(BF16)
    row = lax.broadcasted_iota(jnp.int32, qt.shape, 0)
    zero = jnp.zeros_like(qt)
    qt1 = jnp.where(row < HEAD_DIM, qt, zero)
    qt2 = jnp.where(row >= HEAD_DIM, qt, zero)
    acc1_ref[...] = jnp.zeros_like(acc1_ref)
    acc2_ref[...] = jnp.zeros_like(acc2_ref)
    stat_ref[...] = jnp.full(stat_ref.shape, NEG_BIG, F32)

    def scores(j, s_ref):
        kj = kb_ref[pl.ds(pl.multiple_of(j * tq, tq), tq), :]
        s_ref[0] = jnp.dot(kj, qt1, preferred_element_type=F32)
        s_ref[1] = jnp.dot(kj, qt2, preferred_element_type=F32)

    def softmax_pv(j, s_ref, diagonal):
        vtj = vt_ref[j]
        if diagonal:
            key_chunk = lax.broadcasted_iota(jnp.int32, (tq, tq), 0) // CHUNK
            qry_chunk = lax.broadcasted_iota(jnp.int32, (tq, tq), 1) // CHUNK
            mask = key_chunk <= qry_chunk
        for c, acc_ref in ((0, acc1_ref), (1, acc2_ref)):
            s = s_ref[c]
            if diagonal:
                s = jnp.where(mask, s, NEG_BIG)
            m = stat_ref[c:c + 1, :]
            mn = jnp.maximum(m, jnp.max(s, axis=0, keepdims=True))
            alpha = jnp.exp(m - mn)
            p = jnp.exp(s - mn)
            stat_ref[c:c + 1, :] = mn
            acc_ref[...] = alpha * acc_ref[...] + jnp.dot(vtj, p.astype(BF16), preferred_element_type=F32)

    scores(0, sa_ref)

    def pair(p, _):
        t = 2 * p
        scores(t + 1, sb_ref)
        softmax_pv(t, sa_ref, False)
        scores(t + 2, sa_ref)
        softmax_pv(t + 1, sb_ref, False)
        return 0

    lax.fori_loop(0, i // 2, pair, 0)

    @pl.when(i % 2 == 0)
    def _():
        softmax_pv(i, sa_ref, True)

    @pl.when(i % 2 == 1)
    def _():
        scores(i, sb_ref)
        softmax_pv(i - 1, sa_ref, False)
        softmax_pv(i, sb_ref, True)

    lam = _lambda(lam_ref)
    l1 = acc1_ref[V_DIM:V_DIM + 1, :]
    l2 = acc2_ref[V_DIM:V_DIM + 1, :]
    o = acc1_ref[0:V_DIM, :] / l1 - lam * (acc2_ref[0:V_DIM, :] / l2)
    ms = jnp.mean(o * o, axis=0, keepdims=True)
    o = ((o * lax.rsqrt(ms + RMS_EPS)) * g_ref[...]) * (1.0 - LAM_INIT)
    o_ref[...] = o.T.astype(BF16)


def _attn_prompt(q, k, v, lam_vecs, subln_g, n_heads):
    B, T, _ = q.shape
    tq = min(T, 512)
    assert T % tq == 0 and tq % LANES == 0
    qblk = pl.BlockSpec((None, tq, V_DIM), lambda b, h, i: (b, i, h))
    kvblk = pl.BlockSpec((None, T, V_DIM), lambda b, h, i: (b, 0, h))
    return pl.pallas_call(
        _attn_prompt_kernel,
        grid=(B, n_heads, T // tq),
        in_specs=[qblk, kvblk, kvblk, _const_spec(lam_vecs.shape), _const_spec((V_DIM, 1))],
        out_specs=qblk,
        out_shape=jax.ShapeDtypeStruct((B, T, n_heads * V_DIM), BF16),
        scratch_shapes=[pltpu.VMEM((T, V_DIM), BF16), pltpu.VMEM((T // tq, V_DIM + ONES_ROWS, tq), BF16),
                        pltpu.VMEM((V_DIM + ONES_ROWS, tq), F32), pltpu.VMEM((V_DIM + ONES_ROWS, tq), F32),
                        pltpu.VMEM((2, tq), F32),
                        pltpu.VMEM((2, tq, tq), F32), pltpu.VMEM((2, tq, tq), F32)],
        compiler_params=_params(("arbitrary", "arbitrary", "arbitrary")),
        name="attn_prompt",
    )(q, k, v, lam_vecs, subln_g.reshape(V_DIM, 1))


def _attn_sample_kernel(q_ref, ck_ref, cv_ref, nk_ref, nv_ref, lam_ref, g_ref, o_ref, *, past):
    S = q_ref.shape[0]
    P = ck_ref.shape[0]
    q1, q2 = _split_maps(q_ref[...])
    ck = ck_ref[...].astype(BF16)
    nk = nk_ref[...].astype(BF16)
    cv = cv_ref[...].astype(BF16)
    nv = nv_ref[...].astype(BF16)
    q_chunk = (past + lax.broadcasted_iota(jnp.int32, (S, 1), 0)) // CHUNK
    mask_c = (lax.broadcasted_iota(jnp.int32, (S, P), 1) // CHUNK) <= q_chunk
    mask_n = ((past + lax.broadcasted_iota(jnp.int32, (S, S), 1)) // CHUNK) <= q_chunk
    outs = []
    for qm in (q1, q2):
        sc = jnp.where(mask_c, _nt_dot(qm, ck), NEG_BIG)
        sn = jnp.where(mask_n, _nt_dot(qm, nk), NEG_BIG)
        m = jnp.maximum(jnp.max(sc, axis=-1, keepdims=True), jnp.max(sn, axis=-1, keepdims=True))
        pc = jnp.exp(sc - m)
        pn = jnp.exp(sn - m)
        l = jnp.sum(pc, axis=-1, keepdims=True) + jnp.sum(pn, axis=-1, keepdims=True)
        acc = (jnp.dot(pc.astype(BF16), cv, preferred_element_type=F32)
               + jnp.dot(pn.astype(BF16), nv, preferred_element_type=F32))
        outs.append(acc / l)
    o = outs[0] - _lambda(lam_ref) * outs[1]
    o_ref[...] = _head_out(o, g_ref[...]).astype(BF16)


def _attn_sample(q, k, v, cache_k, cache_v, lam_vecs, subln_g, n_heads):
    B, S, _ = q.shape
    P = cache_k.shape[1]
    nblk = pl.BlockSpec((None, S, V_DIM), lambda b, h: (b, 0, h))
    cblk = pl.BlockSpec((None, P, V_DIM), lambda b, h: (b, 0, h))
    kern = functools.partial(_attn_sample_kernel, past=P)
    return pl.pallas_call(
        kern,
        grid=(B, n_heads),
        in_specs=[nblk, cblk, cblk, nblk, nblk, _const_spec(lam_vecs.shape), _const_spec((1, V_DIM))],
        out_specs=nblk,
        out_shape=jax.ShapeDtypeStruct((B, S, n_heads * V_DIM), BF16),
        compiler_params=_params(("arbitrary", "arbitrary")),
        name="attn_sample",
    )(q, cache_k, cache_v, k, v, lam_vecs, subln_g.reshape(1, V_DIM))


def _merge_kernel(x_ref, py_ref, at_ref, gmix_ref, wg_ref, bg_ref, wpb_ref, wab_ref, wo_ref,
                  gffn_ref, wr_ref, br_ref, tri_ref,
                  x1_ref, ri_ref, wc_ref, cnt_ref, wrow_ref, run_ref):
    step = pl.program_id(0)
    tm, D = x_ref.shape

    @pl.when(step == 0)
    def _():
        wrow_ref[...] = jnp.zeros_like(wrow_ref)
        run_ref[...] = jnp.zeros_like(run_ref)

    x = x_ref[...]
    h = _rms(x, gmix_ref[...]).astype(BF16)
    gl = jnp.dot(h, wg_ref[...], preferred_element_type=F32) + bg_ref[...]
    a = jnp.dot(py_ref[...], wpb_ref[...], preferred_element_type=F32)
    bb = jnp.dot(at_ref[...], wab_ref[...], preferred_element_type=F32)
    merged = jax.nn.sigmoid(gl[:, :D]) * a + jax.nn.sigmoid(gl[:, D:]) * bb
    x1 = x + jnp.dot(merged.astype(BF16), wo_ref[...], preferred_element_type=F32)
    x1_ref[...] = x1

    h2 = _rms(x1, gffn_ref[...]).astype(BF16)
    logits = jnp.dot(h2, wr_ref[...], preferred_element_type=F32) + br_ref[...]
    lt = logits.T
    G = N_EXPERT_GROUPS
    E = EXPERTS_PER_GROUP
    best = lt[0:1]
    gsel = jnp.zeros((1, tm), jnp.int32)
    for g in range(1, G):
        better = lt[g:g + 1] > best
        gsel = jnp.where(better, g, gsel)
        best = jnp.where(better, lt[g:g + 1], best)
    gden = jnp.zeros((1, tm), F32)
    for g in range(G):
        gden = gden + jnp.exp(lt[g:g + 1] - best)
    g_w = 1.0 / gden
    el = [lt[G + e:G + e + 1] for e in range(E)]
    for g in range(1, G):
        sel = gsel == g
        el = [jnp.where(sel, lt[G + g * E + e:G + g * E + e + 1], el[e]) for e in range(E)]
    v0 = el[0]
    i0 = jnp.zeros((1, tm), jnp.int32)
    for e in range(1, E):
        better = el[e] > v0
        i0 = jnp.where(better, e, i0)
        v0 = jnp.where(better, el[e], v0)
    v1 = jnp.full((1, tm), -jnp.inf, F32)
    i1 = jnp.zeros((1, tm), jnp.int32)
    for e in range(E):
        better = (i0 != e) & (el[e] > v1)
        i1 = jnp.where(better, e, i1)
        v1 = jnp.where(better, el[e], v1)
    ex = jnp.exp(v1 - v0)
    w0 = (1.0 / (1.0 + ex)) * g_w
    w1 = (ex / (1.0 + ex)) * g_w
    e0 = gsel * E + i0
    e1 = gsel * E + i1

    eid = lax.broadcasted_iota(jnp.int32, (N_EXPERTS, tm), 0)
    oh0 = eid == e0
    oh1 = eid == e1
    oh = jnp.where(oh0 | oh1, 1.0, 0.0).astype(BF16)
    pref = jnp.dot(oh, tri_ref[...], preferred_element_type=F32)
    rank = run_ref[...] + pref - 1.0
    r0 = jnp.sum(jnp.where(oh0, rank, 0.0), axis=0, keepdims=True)
    r1 = jnp.sum(jnp.where(oh1, rank, 0.0), axis=0, keepdims=True)
    run_ref[...] = run_ref[...] + pref[:, tm - 1:tm]
    cnt_ref[...] = jnp.broadcast_to(run_ref[...], cnt_ref.shape).astype(jnp.int32)

    ri_ref[0:1, :] = e0
    ri_ref[1:2, :] = e1
    ri_ref[2:3, :] = r0.astype(jnp.int32)
    ri_ref[3:4, :] = r1.astype(jnp.int32)
    wrow_ref[0:1, :] = w0
    wrow_ref[1:2, :] = w1
    wc_ref[...] = wrow_ref[...].T


def _merge(x, py, at, p):
    N, D = x.shape
    tm = min(N, 512)
    assert N % tm == 0
    tri = (jnp.arange(tm)[:, None] <= jnp.arange(tm)[None, :]).astype(BF16)
    row = lambda w: pl.BlockSpec((tm, w), lambda i: (i, 0))
    consts = [p["gmix"], p["w_gate"], p["b_gate"], p["w_pb"], p["w_ab"], p["w_out"],
              p["gffn"], p["w_router"], p["b_router"], tri]
    return pl.pallas_call(
        _merge_kernel,
        grid=(N // tm,),
        in_specs=[row(D), row(py.shape[1]), row(at.shape[1])] + [_const_spec(c.shape) for c in consts],
        out_specs=[row(D), pl.BlockSpec((4, tm), lambda i: (0, i)), row(LANES),
                   _const_spec((N_EXPERTS, LANES))],
        out_shape=[jax.ShapeDtypeStruct((N, D), F32),
                   jax.ShapeDtypeStruct((4, N), jnp.int32),
                   jax.ShapeDtypeStruct((N, LANES), F32),
                   jax.ShapeDtypeStruct((N_EXPERTS, LANES), jnp.int32)],
        scratch_shapes=[pltpu.VMEM((LANES, tm), F32), pltpu.VMEM((N_EXPERTS, 1), F32)],
        compiler_params=_params(("arbitrary",)),
        name="merge",
    )(x, py, at, *consts)


def _row_tiles_store(ref_view, x):
    rows = x.shape[0]
    for s in range(x.shape[1] // LANES):
        ref_view[pl.ds(s, rows, stride=x.shape[1] // LANES), :] = x[:, s * LANES:(s + 1) * LANES]


def _row_tiles_load(ref_view, rows, width):
    n = width // LANES
    return jnp.concatenate([ref_view[pl.ds(s, rows, stride=n), :] for s in range(n)], axis=1)


def _dispatch_kernel(pos_ref, x1_ref, g_ref, xs_ref, buf_ref, sem_ref):
    i = pl.program_id(0)
    n = pl.num_programs(0)
    td, D = x1_ref.shape
    rt = D // LANES
    slot = i % 2

    def drain(s):
        for _ in range(2):
            pltpu.make_async_copy(buf_ref.at[s], xs_ref.at[pl.ds(0, td * rt)], sem_ref.at[s]).wait()

    @pl.when(i >= 2)
    def _():
        drain(slot)

    _row_tiles_store(buf_ref.at[slot], _rms(x1_ref[...], g_ref[...]))

    def issue(r8, _):
        for u in range(ISSUE_UNROLL):
            r = r8 * ISSUE_UNROLL + u
            for c in range(2):
                dst = pl.multiple_of(pos_ref[0, c * td + r] * rt, rt)
                pltpu.make_async_copy(buf_ref.at[slot, pl.ds(r * rt, rt)], xs_ref.at[pl.ds(dst, rt)],
                                      sem_ref.at[slot]).start(priority=c)
        return 0

    lax.fori_loop(0, td // ISSUE_UNROLL, issue, 0)

    @pl.when(i == n - 1)
    def _():
        drain(slot)

        @pl.when(n >= 2)
        def _():
            drain(1 - slot)


def _dispatch(x1, pos, gffn):
    N, D = x1.shape
    td = min(N, 512)
    assert N % td == 0 and td % ISSUE_UNROLL == 0
    nt = N // td
    rt = D // LANES
    pos3 = pos.reshape(2, nt, td).transpose(1, 0, 2).reshape(nt, 1, 2 * td)
    return pl.pallas_call(
        _dispatch_kernel,
        grid=(nt,),
        in_specs=[pl.BlockSpec((None, 1, 2 * td), lambda i: (i, 0, 0), memory_space=pltpu.SMEM),
                  pl.BlockSpec((td, D), lambda i: (i, 0)), _const_spec((1, D))],
        out_specs=pl.BlockSpec(memory_space=pl.ANY),
        out_shape=jax.ShapeDtypeStruct((2 * N * rt, LANES), F32),
        scratch_shapes=[pltpu.VMEM((2, td * rt, LANES), F32), pltpu.SemaphoreType.DMA((2,))],
        compiler_params=_params(("arbitrary",)),
        name="dispatch",
    )(pos3, x1, gffn)


def _ffn_kernel(tile_ref, ex_ref, lo_ref, hi_ref, first_ref, xs_ref, wg_ref, wu_ref, wd_ref, ys_ref):
    i = pl.program_id(0)
    D = wg_ref.shape[0]
    rt = D // LANES
    tf = xs_ref.shape[0] // rt
    lo = lo_ref[i]
    hi = hi_ref[i]

    @pl.when(hi > lo)
    def _():
        xb = _row_tiles_load(xs_ref, tf, D).astype(BF16)
        gate = jnp.dot(xb, wg_ref[...], preferred_element_type=F32)
        up = jnp.dot(xb, wu_ref[...], preferred_element_type=F32)
        hid = (jax.nn.silu(gate) * up).astype(BF16)
        y = jnp.dot(hid, wd_ref[...], preferred_element_type=F32)
        rows = tile_ref[i] * tf + lax.broadcasted_iota(jnp.int32, (tf, 1), 0)
        mine = (rows >= lo) & (rows < hi)

        @pl.when(first_ref[i] == 1)
        def _():
            _row_tiles_store(ys_ref, jnp.where(mine, y, 0.0))

        @pl.when(first_ref[i] == 0)
        def _():
            _row_tiles_store(ys_ref, jnp.where(mine, y, _row_tiles_load(ys_ref, tf, D)))


def _ffn(xs, sched, w_gate, w_up, w_down, tf):
    D, Hd = w_gate.shape[1:]
    rt = D // LANES
    n_items = sched[0].shape[0]
    rows = pl.BlockSpec((tf * rt, LANES), lambda i, tile, ex, lo, hi, first: (tile[i], 0))
    wspec = lambda a, b_: pl.BlockSpec((None, a, b_), lambda i, tile, ex, lo, hi, first: (ex[i], 0, 0))
    return pl.pallas_call(
        _ffn_kernel,
        grid_spec=pltpu.PrefetchScalarGridSpec(
            num_scalar_prefetch=5,
            grid=(n_items,),
            in_specs=[rows, wspec(D, Hd), wspec(D, Hd), wspec(Hd, D)],
            out_specs=rows,
        ),
        out_shape=jax.ShapeDtypeStruct(xs.shape, F32),
        compiler_params=_params(("arbitrary",)),
        name="expert_ffn",
    )(*sched, xs, w_gate, w_up, w_down)


def _schedule(counts, M, tf):
    n_tiles = M // tf
    n_items = n_tiles + N_EXPERTS - 1
    ends = jnp.cumsum(counts)
    starts = ends - counts
    first_tile = starts // tf
    last_tile = jnp.maximum(ends - 1, 0) // tf
    per = jnp.where(counts > 0, last_tile - first_tile + 1, 0)
    item_end = jnp.cumsum(per)
    item_start = item_end - per
    total = item_end[-1]
    idx = jnp.arange(n_items, dtype=jnp.int32)
    valid = idx < total
    idx_c = jnp.minimum(idx, total - 1)
    ex = jnp.minimum(jnp.sum(item_end[None, :] <= idx_c[:, None], axis=1), N_EXPERTS - 1).astype(jnp.int32)
    onehot = ex[:, None] == jnp.arange(N_EXPERTS)[None, :]
    pick = lambda table: jnp.sum(jnp.where(onehot, table[None, :], 0), axis=1)
    tile = (pick(first_tile) + idx_c - pick(item_start)).astype(jnp.int32)
    lo = jnp.where(valid, pick(starts), 0).astype(jnp.int32)
    hi = jnp.where(valid, pick(ends), 0).astype(jnp.int32)
    prev = jnp.concatenate([jnp.full((1,), -1, jnp.int32), tile[:-1]])
    first = (valid & (tile != prev)).astype(jnp.int32)
    return (tile, ex, lo, hi, first), starts


def _combine_kernel(pos_ref, nxt_ref, x1_ref, wc_ref, g_ref, ys_ref, o_ref, buf_ref, sem_ref):
    i = pl.program_id(0)
    n = pl.num_programs(0)
    tc, D = x1_ref.shape
    rt = D // LANES
    slot = i % 2

    def gather(idx_ref, s):
        def issue(r8, _):
            for u in range(ISSUE_UNROLL):
                r = r8 * ISSUE_UNROLL + u
                for c in range(2):
                    src = pl.multiple_of(idx_ref[0, c * tc + r] * rt, rt)
                    pltpu.make_async_copy(ys_ref.at[pl.ds(src, rt)], buf_ref.at[s, c, pl.ds(r * rt, rt)],
                                          sem_ref.at[s]).start(priority=c)
            return 0
        lax.fori_loop(0, tc // ISSUE_UNROLL, issue, 0)

    @pl.when(i == 0)
    def _():
        gather(pos_ref, slot)

    @pl.when(i + 1 < n)
    def _():
        gather(nxt_ref, 1 - slot)

    for c in range(2):
        pltpu.make_async_copy(ys_ref.at[pl.ds(0, tc * rt)], buf_ref.at[slot, c], sem_ref.at[slot]).wait()

    wc = wc_ref[...]
    r0 = _row_tiles_load(buf_ref.at[slot, 0], tc, D)
    r1 = _row_tiles_load(buf_ref.at[slot, 1], tc, D)
    y = x1_ref[...] + (wc[:, 0:1] * r0 + wc[:, 1:2] * r1)
    o_ref[...] = _rms(y, g_ref[...])


def _combine(x1, pos, wcol, ys, final_g):
    N, D = x1.shape
    tc = min(N, 256)
    assert N % tc == 0 and tc % ISSUE_UNROLL == 0
    nt = N // tc
    rt = D // LANES
    pos3 = pos.reshape(2, nt, tc).transpose(1, 0, 2).reshape(nt, 1, 2 * tc)
    smem = lambda f: pl.BlockSpec((None, 1, 2 * tc), f, memory_space=pltpu.SMEM)
    return pl.pallas_call(
        _combine_kernel,
        grid=(nt,),
        in_specs=[smem(lambda i: (i, 0, 0)), smem(lambda i: (jnp.minimum(i + 1, nt - 1), 0, 0)),
                  pl.BlockSpec((tc, D), lambda i: (i, 0)), pl.BlockSpec((tc, LANES), lambda i: (i, 0)),
                  _const_spec((1, D)), pl.BlockSpec(memory_space=pl.ANY)],
        out_specs=pl.BlockSpec((tc, D), lambda i: (i, 0)),
        out_shape=jax.ShapeDtypeStruct((N, D), F32),
        scratch_shapes=[pltpu.VMEM((2, 2, tc * rt, LANES), F32), pltpu.SemaphoreType.DMA((2,))],
        compiler_params=_params(("arbitrary",)),
        name="combine",
    )(pos3, pos3, x1, wcol, final_g, ys)


def _moe_and_norm(x1, ri, wcol, counts, p):
    N, D = x1.shape
    tf = min(2 * N, 512)
    sched, starts = _schedule(counts, 2 * N, tf)
    hit = ri[0:2][:, None, :] == jnp.arange(N_EXPERTS, dtype=jnp.int32)[None, :, None]
    pos = jnp.sum(jnp.where(hit, starts.astype(jnp.int32)[None, :, None], 0), axis=1) + ri[2:4]
    xs = _dispatch(x1, pos, p["gffn"])
    ys = _ffn(xs, sched, p["w_eg"], p["w_eu"], p["w_ed"], tf)
    return _combine(x1, pos, wcol, ys, p["final_g"])


def _stream(x, pos0, pool_hdr, cache, p, n_heads):
    B, T, D = x.shape
    q, k, v, py, npool = _inproj(x, pool_hdr, pos0, p["gmix"], p["w_a"], p["pool_w"], p["pool_scale"], n_heads)
    if cache is None:
        at = _attn_prompt(q, k, v, p["lam_vecs"], p["subln_g"], n_heads)
    else:
        at = _attn_sample(q, k, v, cache[0], cache[1], p["lam_vecs"], p["subln_g"], n_heads)
    N = B * T
    x1, ri, wcol, cnt = _merge(x.reshape(N, D), py.reshape(N, -1), at.reshape(N, -1), p)
    y = _moe_and_norm(x1, ri, wcol, cnt[:, 0], p)
    return (y.reshape(B, T, D), k.reshape(1, B, T, n_heads, 2 * HEAD_DIM), v.reshape(1, B, T, n_heads, V_DIM),
            npool[None, :, 1:, :])


def kernel(x_prompt, x_sample, cache_k, cache_v, state_pool, norm_mix_g, w_in, b_gate, lambda_q1, lambda_k1,
           lambda_q2, lambda_k2, subln_g, pool_w, pool_scale, w_pool_branch, w_attn_branch, w_out, norm_ffn_g,
           w_router_group, b_router_group, w_router_expert, b_router_expert, w_expert_gate, w_expert_up,
           w_expert_down, final_norm_g):
    assert w_in.shape[0] == 1, "single layer"
    D = x_prompt.shape[-1]
    n_heads = cache_k.shape[3]
    P = pool_scale.shape[-1]
    past = cache_k.shape[2]
    n_a = P + 2 * n_heads * 2 * HEAD_DIM + n_heads * V_DIM

    wr = jnp.concatenate([w_router_group[0], jnp.transpose(w_router_expert[0], (1, 0, 2)).reshape(D, N_EXPERTS)],
                         axis=1)
    br = jnp.concatenate([b_router_group[0], b_router_expert[0].reshape(N_EXPERTS)])
    n_r = wr.shape[1]
    p = dict(
        gmix=norm_mix_g[0].reshape(1, D),
        w_a=w_in[0, :, :n_a].astype(BF16),
        w_gate=w_in[0, :, n_a:].astype(BF16),
        b_gate=b_gate[0].reshape(1, 2 * D),
        lam_vecs=jnp.stack([lambda_q1[0], lambda_k1[0], lambda_q2[0], lambda_k2[0]]),
        subln_g=subln_g[0],
        pool_w=pool_w[0].astype(BF16),
        pool_scale=pool_scale[0],
        w_pb=w_pool_branch[0].astype(BF16),
        w_ab=w_attn_branch[0].astype(BF16),
        w_out=w_out[0].astype(BF16),
        gffn=norm_ffn_g[0].reshape(1, D),
        w_router=jnp.pad(wr, ((0, 0), (0, LANES - n_r))).astype(BF16),
        b_router=jnp.pad(br, (0, LANES - n_r)).reshape(1, LANES),
        w_eg=w_expert_gate[0].astype(BF16),
        w_eu=w_expert_up[0].astype(BF16),
        w_ed=w_expert_down[0].astype(BF16),
        final_g=final_norm_g.reshape(1, D),
    )

    Bp = x_prompt.shape[0]
    Bs = x_sample.shape[0]
    hdr_p = jnp.zeros((Bp, POOL_HDR, P), F32)
    hdr_s = jnp.pad(state_pool[0], ((0, 0), (1, 0), (0, 0)))
    cache = (cache_k[0].reshape(Bs, past, -1), cache_v[0].reshape(Bs, past, -1))

    yp, kp, vp, pp = _stream(x_prompt, 0, hdr_p, None, p, n_heads)
    ys, ks, vs, ps = _stream(x_sample, past, hdr_s, cache, p, n_heads)
    return (yp, ys, kp, vp, pp, ks, vs, ps)
```

```python
import functools
import math

import jax
import jax.numpy as jnp
from jax import lax
from jax.experimental import pallas as pl
from jax.experimental.pallas import tpu as pltpu

CHUNK = 64
HEAD_DIM = 64
V_DIM = 2 * HEAD_DIM
ROT_DIM = HEAD_DIM // 4
ROPE_THETA = 500000.0
ATTN_SCALE = HEAD_DIM ** -0.5
POOL_WINDOWS = (2, 4, 8, 16)
POOL_GROUP_DIM = 128
POOL_BUF = max(POOL_WINDOWS) - 1
POOL_HDR = POOL_BUF + 1
N_EXPERT_GROUPS = 4
EXPERTS_PER_GROUP = 4
N_EXPERTS = N_EXPERT_GROUPS * EXPERTS_PER_GROUP
RMS_EPS = 1e-6
LAM_INIT = 0.8 - 0.6 * math.exp(-0.3 * 0)

LANES = 128
ISSUE_UNROLL = 8
ONES_ROWS = 16
VMEM_LIMIT = 56 * 1024 * 1024
NEG_BIG = -1e30

F32 = jnp.float32
BF16 = jnp.bfloat16


def _params(semantics):
    return pltpu.CompilerParams(dimension_semantics=semantics, vmem_limit_bytes=VMEM_LIMIT)


def _const_spec(shape):
    nd = len(shape)
    return pl.BlockSpec(shape, lambda *_: (0,) * nd)


def _rms(x, g):
    ms = jnp.mean(x * x, axis=-1, keepdims=True)
    return (x * lax.rsqrt(ms + RMS_EPS)) * g


def _inproj_kernel(x_ref, g_ref, w_ref, cos_ref, sa_ref, sb_ref, pbuf_ref, pw_ref, ps_ref,
                   q_ref, k_ref, v_ref, py_ref, npool_ref, ext_ref, carry_ref, *, pos0, n_heads):
    t = pl.program_id(0)
    b = pl.program_id(1)
    tm = x_ref.shape[0]
    pool_w = ext_ref.shape[1]
    qk_w = n_heads * 2 * HEAD_DIM

    h = _rms(x_ref[...], g_ref[...]).astype(BF16)

    @pl.when(t == 0)
    def _():
        ext_ref[0:POOL_HDR, :] = pbuf_ref[...]

    @pl.when(t > 0)
    def _():
        ext_ref[0:POOL_HDR, :] = carry_ref[b]

    ext_ref[POOL_HDR:, :] = jnp.dot(h, w_ref[:, 0:pool_w], preferred_element_type=F32)
    tail = ext_ref[tm:tm + POOL_HDR, :]
    carry_ref[b] = tail
    npool_ref[...] = tail

    cos = cos_ref[...]
    sa = sa_ref[...]
    sb = sb_ref[...]
    col_chunk = 4 * LANES
    for which, out_ref in ((0, q_ref), (1, k_ref)):
        for c in range(0, qk_w, col_chunk):
            base = pool_w + which * qk_w + c
            z = jnp.dot(h, w_ref[:, base:base + col_chunk], preferred_element_type=F32)
            for j in range(0, col_chunk, LANES):
                zc = z[:, j:j + LANES]
                rot = (zc * cos + pltpu.roll(zc, LANES - ROT_DIM // 2, 1) * sa
                       + pltpu.roll(zc, ROT_DIM // 2, 1) * sb)
                if which == 0:
                    out_ref[:, c + j:c + j + LANES] = (rot * ATTN_SCALE).astype(BF16)
                else:
                    out_ref[:, c + j:c + j + LANES] = rot

    v_w = v_ref.shape[1]
    for c in range(0, v_w, col_chunk):
        base = pool_w + 2 * qk_w + c
        v_ref[:, c:c + col_chunk] = jnp.dot(h, w_ref[:, base:base + col_chunk], preferred_element_type=F32)

    pos = pos0 + t * tm + lax.broadcasted_iota(jnp.int32, (tm, 1), 0)
    for g, w in enumerate(POOL_WINDOWS):
        sl = slice(g * POOL_GROUP_DIM, (g + 1) * POOL_GROUP_DIM)
        u = ext_ref[POOL_HDR:POOL_HDR + tm, sl]
        ws = u
        for j in range(1, w):
            ws = ws + ext_ref[POOL_HDR - j:POOL_HDR - j + tm, sl]
        cnt = jnp.minimum(w, pos + 1).astype(F32)
        d = (ws / cnt - u).astype(BF16)
        y = jnp.dot(d, pw_ref[g], preferred_element_type=F32) * ps_ref[:, sl]
        py_ref[:, sl] = y.astype(BF16)


def _inproj(x, pool_hdr, pos0, norm_g, w_a, pool_w, pool_scale, n_heads):
    B, T, D = x.shape
    P = pool_scale.shape[-1]
    qk_w = n_heads * 2 * HEAD_DIM
    v_w = n_heads * V_DIM
    tm = min(T, 512)
    assert T % tm == 0 and T >= POOL_HDR and tm % 16 == 0
    nt = T // tm

    pos = pos0 + jnp.arange(T)
    inv_freq = jnp.power(ROPE_THETA, -jnp.arange(0, ROT_DIM, 2, dtype=F32) / ROT_DIM)
    ang = pos.astype(F32)[:, None] * inv_freq[None, :]
    cos, sin = jnp.cos(ang), jnp.sin(ang)
    half = ROT_DIM // 2
    ones = jnp.ones((T, HEAD_DIM - ROT_DIM), F32)
    zeros_h = jnp.zeros((T, half), F32)
    zeros_r = jnp.zeros((T, HEAD_DIM - ROT_DIM), F32)
    cos_t = jnp.tile(jnp.concatenate([cos, cos, ones], axis=1), (1, LANES // HEAD_DIM))
    sa_t = jnp.tile(jnp.concatenate([-sin, zeros_h, zeros_r], axis=1), (1, LANES // HEAD_DIM))
    sb_t = jnp.tile(jnp.concatenate([zeros_h, sin, zeros_r], axis=1), (1, LANES // HEAD_DIM))

    row = lambda w: pl.BlockSpec((None, tm, w), lambda t, b: (b, t, 0))
    tab = pl.BlockSpec((tm, LANES), lambda t, b: (t, 0))
    hdr = pl.BlockSpec((None, POOL_HDR, P), lambda t, b: (b, 0, 0))
    hdr_out = pl.BlockSpec((None, POOL_HDR, P), lambda t, b: (jnp.where(t == nt - 1, b, 0), 0, 0))
    kern = functools.partial(_inproj_kernel, pos0=pos0, n_heads=n_heads)
    return pl.pallas_call(
        kern,
        grid=(nt, B),
        in_specs=[row(D), _const_spec((1, D)), _const_spec(w_a.shape), tab, tab, tab, hdr,
                  _const_spec(pool_w.shape), _const_spec((1, P))],
        out_specs=[row(qk_w), row(qk_w), row(v_w), row(P), hdr_out],
        out_shape=[jax.ShapeDtypeStruct((B, T, qk_w), BF16),
                   jax.ShapeDtypeStruct((B, T, qk_w), F32),
                   jax.ShapeDtypeStruct((B, T, v_w), F32),
                   jax.ShapeDtypeStruct((B, T, P), BF16),
                   jax.ShapeDtypeStruct((B, POOL_HDR, P), F32)],
        scratch_shapes=[pltpu.VMEM((POOL_HDR + tm, P), F32), pltpu.VMEM((B, POOL_HDR, P), F32)],
        compiler_params=_params(("arbitrary", "arbitrary")),
        name="inproj",
    )(x, norm_g.reshape(1, D), w_a, cos_t, sa_t, sb_t, pool_hdr, pool_w, pool_scale.reshape(1, P))


def _lambda(lam_ref):
    lv = lam_ref[...]
    a = jnp.sum(lv[0:1] * lv[1:2], axis=-1, keepdims=True)
    c = jnp.sum(lv[2:3] * lv[3:4], axis=-1, keepdims=True)
    return jnp.exp(a) - jnp.exp(c) + LAM_INIT


def _split_maps(q):
    lane = lax.broadcasted_iota(jnp.int32, q.shape, 1)
    zero = jnp.zeros_like(q)
    return jnp.where(lane < HEAD_DIM, q, zero), jnp.where(lane >= HEAD_DIM, q, zero)


def _nt_dot(a, b):
    return lax.dot_general(a, b, (((1,), (1,)), ((), ())), preferred_element_type=F32)


def _head_out(o, g):
    ms = jnp.mean(o * o, axis=-1, keepdims=True)
    return ((o * lax.rsqrt(ms + RMS_EPS)) * g) * (1.0 - LAM_INIT)


def _attn_prompt_kernel(q_ref, k_ref, v_ref, lam_ref, g_ref, o_ref,
                        kb_ref, vt_ref, acc1_ref, acc2_ref, stat_ref, sa_ref, sb_ref, sc_ref, *, tq):
    T = q_ref.shape[0]
    nq = T // tq

    kb_ref[...] = k_ref[...].astype(BF16)
    ones_row = jnp.where(lax.broadcasted_iota(jnp.int32, (ONES_ROWS, tq), 0) == 0, 1.0, 0.0).astype(BF16)
    for c in range(nq):
        vt_ref[c, 0:V_DIM, :] = v_ref[c * tq:(c + 1) * tq, :].T.astype(BF16)
        vt_ref[c, V_DIM:, :] = ones_row

    def q_maps(i):
        qt = q_ref[pl.ds(pl.multiple_of(i * tq, tq), tq), :].astype(F32).T.astype(BF16)
        row = lax.broadcasted_iota(jnp.int32, qt.shape, 0)
        zero = jnp.zeros_like(qt)
        return jnp.where(row < HEAD_DIM, qt, zero), jnp.where(row >= HEAD_DIM, qt, zero)

    def scores(j, s_ref, qm):
        kj = kb_ref[pl.ds(pl.multiple_of(j * tq, tq), tq), :]
        s_ref[0] = jnp.dot(kj, qm[0], preferred_element_type=F32)
        s_ref[1] = jnp.dot(kj, qm[1], preferred_element_type=F32)

    def softmax_pv(j, s_ref, diagonal):
        vtj = vt_ref[j]
        if diagonal:
            key_chunk = lax.broadcasted_iota(jnp.int32, (tq, tq), 0) // CHUNK
            qry_chunk = lax.broadcasted_iota(jnp.int32, (tq, tq), 1) // CHUNK
            mask = key_chunk <= qry_chunk
        for c, acc_ref in ((0, acc1_ref), (1, acc2_ref)):
            s = s_ref[c]
            if diagonal:
                s = jnp.where(mask, s, NEG_BIG)
            m = stat_ref[c:c + 1, :]
            mn = jnp.maximum(m, jnp.max(s, axis=0, keepdims=True))
            alpha = jnp.exp(m - mn)
            p = jnp.exp(s - mn)
            stat_ref[c:c + 1, :] = mn
            acc_ref[...] = alpha * acc_ref[...] + jnp.dot(vtj, p.astype(BF16), preferred_element_type=F32)

    lam = _lambda(lam_ref)
    scores(0, sc_ref, q_maps(0))

    def q_tile(i, _):
        qm = q_maps(i)
        qn = q_maps(jnp.minimum(i + 1, nq - 1))
        acc1_ref[...] = jnp.zeros_like(acc1_ref)
        acc2_ref[...] = jnp.zeros_like(acc2_ref)
        stat_ref[...] = jnp.full(stat_ref.shape, NEG_BIG, F32)

        @pl.when(i == 0)
        def _():
            softmax_pv(0, sc_ref, True)
            scores(0, sc_ref, qn)

        @pl.when(i > 0)
        def _():
            scores(1, sa_ref, qm)
            softmax_pv(0, sc_ref, False)

        def pair(p, _):
            t = 2 * p + 1
            scores(t + 1, sb_ref, qm)
            softmax_pv(t, sa_ref, False)
            scores(t + 2, sa_ref, qm)
            softmax_pv(t + 1, sb_ref, False)
            return 0

        lax.fori_loop(0, jnp.maximum(i - 1, 0) // 2, pair, 0)

        @pl.when(i % 2 == 1)
        def _():
            scores(0, sc_ref, qn)
            softmax_pv(i, sa_ref, True)

        @pl.when((i % 2 == 0) & (i >= 2))
        def _():
            scores(i, sb_ref, qm)
            softmax_pv(i - 1, sa_ref, False)
            scores(0, sc_ref, qn)
            softmax_pv(i, sb_ref, True)

        l1 = acc1_ref[V_DIM:V_DIM + 1, :]
        l2 = acc2_ref[V_DIM:V_DIM + 1, :]
        o = acc1_ref[0:V_DIM, :] / l1 - lam * (acc2_ref[0:V_DIM, :] / l2)
        ms = jnp.mean(o * o, axis=0, keepdims=True)
        o = ((o * lax.rsqrt(ms + RMS_EPS)) * g_ref[...]) * (1.0 - LAM_INIT)
        o_ref[pl.ds(pl.multiple_of(i * tq, tq), tq), :] = o.T.astype(BF16)
        return 0

    lax.fori_loop(0, nq, q_tile, 0)


def _attn_prompt(q, k, v, lam_vecs, subln_g, n_heads):
    B, T, _ = q.shape
    tq = min(T, 512)
    assert T % tq == 0 and tq % LANES == 0
    blk = pl.BlockSpec((None, T, V_DIM), lambda b, h: (b, 0, h))
    score_buf = pltpu.VMEM((2, tq, tq), F32)
    return pl.pallas_call(
        functools.partial(_attn_prompt_kernel, tq=tq),
        grid=(B, n_heads),
        in_specs=[blk, blk, blk, _const_spec(lam_vecs.shape), _const_spec((V_DIM, 1))],
        out_specs=blk,
        out_shape=jax.ShapeDtypeStruct((B, T, n_heads * V_DIM), BF16),
        scratch_shapes=[pltpu.VMEM((T, V_DIM), BF16), pltpu.VMEM((T // tq, V_DIM + ONES_ROWS, tq), BF16),
                        pltpu.VMEM((V_DIM + ONES_ROWS, tq), F32), pltpu.VMEM((V_DIM + ONES_ROWS, tq), F32),
                        pltpu.VMEM((2, tq), F32), score_buf, score_buf, score_buf],
        compiler_params=_params(("arbitrary", "arbitrary")),
        name="attn_prompt",
    )(q, k, v, lam_vecs, subln_g.reshape(V_DIM, 1))


def _attn_sample_kernel(q_ref, ck_ref, cv_ref, nk_ref, nv_ref, lam_ref, g_ref, o_ref, *, past, n_heads):
    S = q_ref.shape[0]
    P = past
    h = pl.program_id(1)
    q1, q2 = _split_maps(q_ref[...])
    ck = ck_ref[pl.ds(h, P, stride=n_heads), :].astype(BF16)
    cv = cv_ref[pl.ds(h, P, stride=n_heads), :].astype(BF16)
    nk = nk_ref[...].astype(BF16)
    nv = nv_ref[...].astype(BF16)
    q_chunk = (past + lax.broadcasted_iota(jnp.int32, (S, 1), 0)) // CHUNK
    mask_c = (lax.broadcasted_iota(jnp.int32, (S, P), 1) // CHUNK) <= q_chunk
    mask_n = ((past + lax.broadcasted_iota(jnp.int32, (S, S), 1)) // CHUNK) <= q_chunk
    outs = []
    for qm in (q1, q2):
        sc = jnp.where(mask_c, _nt_dot(qm, ck), NEG_BIG)
        sn = jnp.where(mask_n, _nt_dot(qm, nk), NEG_BIG)
        m = jnp.maximum(jnp.max(sc, axis=-1, keepdims=True), jnp.max(sn, axis=-1, keepdims=True))
        pc = jnp.exp(sc - m)
        pn = jnp.exp(sn - m)
        l = jnp.sum(pc, axis=-1, keepdims=True) + jnp.sum(pn, axis=-1, keepdims=True)
        acc = (jnp.dot(pc.astype(BF16), cv, preferred_element_type=F32)
               + jnp.dot(pn.astype(BF16), nv, preferred_element_type=F32))
        outs.append(acc / l)
    o = outs[0] - _lambda(lam_ref) * outs[1]
    o_ref[...] = _head_out(o, g_ref[...]).astype(BF16)


def _attn_sample(q, k, v, cache_k, cache_v, lam_vecs, subln_g, n_heads):
    B, S, _ = q.shape
    P = cache_k.shape[1] // n_heads
    nblk = pl.BlockSpec((None, S, V_DIM), lambda b, h: (b, 0, h))
    cblk = pl.BlockSpec((None, P * n_heads, V_DIM), lambda b, h: (b, 0, 0))
    kern = functools.partial(_attn_sample_kernel, past=P, n_heads=n_heads)
    return pl.pallas_call(
        kern,
        grid=(B, n_heads),
        in_specs=[nblk, cblk, cblk, nblk, nblk, _const_spec(lam_vecs.shape), _const_spec((1, V_DIM))],
        out_specs=nblk,
        out_shape=jax.ShapeDtypeStruct((B, S, n_heads * V_DIM), BF16),
        compiler_params=_params(("arbitrary", "arbitrary")),
        name="attn_sample",
    )(q, cache_k, cache_v, k, v, lam_vecs, subln_g.reshape(1, V_DIM))


def _merge_kernel(x_ref, py_ref, at_ref, gmix_ref, wg_ref, bg_ref, wpb_ref, wab_ref, wo_ref,
                  gffn_ref, wr_ref, br_ref, tri_ref,
                  x1_ref, ri_ref, wc_ref, cnt_ref, wrow_ref, run_ref):
    step = pl.program_id(0)
    tm, D = x_ref.shape

    @pl.when(step == 0)
    def _():
        wrow_ref[...] = jnp.zeros_like(wrow_ref)
        run_ref[...] = jnp.zeros_like(run_ref)

    x = x_ref[...]
    h = _rms(x, gmix_ref[...]).astype(BF16)
    gl = jnp.dot(h, wg_ref[...], preferred_element_type=F32) + bg_ref[...]
    a = jnp.dot(py_ref[...], wpb_ref[...], preferred_element_type=F32)
    bb = jnp.dot(at_ref[...], wab_ref[...], preferred_element_type=F32)
    merged = jax.nn.sigmoid(gl[:, :D]) * a + jax.nn.sigmoid(gl[:, D:]) * bb
    x1 = x + jnp.dot(merged.astype(BF16), wo_ref[...], preferred_element_type=F32)
    x1_ref[...] = x1

    h2 = _rms(x1, gffn_ref[...]).astype(BF16)
    logits = jnp.dot(h2, wr_ref[...], preferred_element_type=F32) + br_ref[...]
    lt = logits.T
    G = N_EXPERT_GROUPS
    E = EXPERTS_PER_GROUP
    best = lt[0:1]
    gsel = jnp.zeros((1, tm), jnp.int32)
    for g in range(1, G):
        better = lt[g:g + 1] > best
        gsel = jnp.where(better, g, gsel)
        best = jnp.where(better, lt[g:g + 1], best)
    gden = jnp.zeros((1, tm), F32)
    for g in range(G):
        gden = gden + jnp.exp(lt[g:g + 1] - best)
    g_w = 1.0 / gden
    el = [lt[G + e:G + e + 1] for e in range(E)]
    for g in range(1, G):
        sel = gsel == g
        el = [jnp.where(sel, lt[G + g * E + e:G + g * E + e + 1], el[e]) for e in range(E)]
    v0 = el[0]
    i0 = jnp.zeros((1, tm), jnp.int32)
    for e in range(1, E):
        better = el[e] > v0
        i0 = jnp.where(better, e, i0)
        v0 = jnp.where(better, el[e], v0)
    v1 = jnp.full((1, tm), -jnp.inf, F32)
    i1 = jnp.zeros((1, tm), jnp.int32)
    for e in range(E):
        better = (i0 != e) & (el[e] > v1)
        i1 = jnp.where(better, e, i1)
        v1 = jnp.where(better, el[e], v1)
    ex = jnp.exp(v1 - v0)
    w0 = (1.0 / (1.0 + ex)) * g_w
    w1 = (ex / (1.0 + ex)) * g_w
    e0 = gsel * E + i0
    e1 = gsel * E + i1

    eid = lax.broadcasted_iota(jnp.int32, (N_EXPERTS, tm), 0)
    oh0 = eid == e0
    oh1 = eid == e1
    oh = jnp.where(oh0 | oh1, 1.0, 0.0).astype(BF16)
    pref = jnp.dot(oh, tri_ref[...], preferred_element_type=F32)
    rank = run_ref[...] + pref - 1.0
    r0 = jnp.sum(jnp.where(oh0, rank, 0.0), axis=0, keepdims=True)
    r1 = jnp.sum(jnp.where(oh1, rank, 0.0), axis=0, keepdims=True)
    run_ref[...] = run_ref[...] + pref[:, tm - 1:tm]
    cnt_ref[...] = jnp.broadcast_to(run_ref[...], cnt_ref.shape).astype(jnp.int32)

    ri_ref[0:1, :] = e0
    ri_ref[1:2, :] = e1
    ri_ref[2:3, :] = r0.astype(jnp.int32)
    ri_ref[3:4, :] = r1.astype(jnp.int32)
    wrow_ref[0:1, :] = w0
    wrow_ref[1:2, :] = w1
    wc_ref[...] = wrow_ref[...].T


def _merge(x, py, at, p):
    N, D = x.shape
    tm = min(N, 512)
    assert N % tm == 0
    tri = (jnp.arange(tm)[:, None] <= jnp.arange(tm)[None, :]).astype(BF16)
    row = lambda w: pl.BlockSpec((tm, w), lambda i: (i, 0))
    consts = [p["gmix"], p["w_gate"], p["b_gate"], p["w_pb"], p["w_ab"], p["w_out"],
              p["gffn"], p["w_router"], p["b_router"], tri]
    return pl.pallas_call(
        _merge_kernel,
        grid=(N // tm,),
        in_specs=[row(D), row(py.shape[1]), row(at.shape[1])] + [_const_spec(c.shape) for c in consts],
        out_specs=[row(D), pl.BlockSpec((4, tm), lambda i: (0, i)), row(LANES),
                   _const_spec((N_EXPERTS, LANES))],
        out_shape=[jax.ShapeDtypeStruct((N, D), F32),
                   jax.ShapeDtypeStruct((4, N), jnp.int32),
                   jax.ShapeDtypeStruct((N, LANES), F32),
                   jax.ShapeDtypeStruct((N_EXPERTS, LANES), jnp.int32)],
        scratch_shapes=[pltpu.VMEM((LANES, tm), F32), pltpu.VMEM((N_EXPERTS, 1), F32)],
        compiler_params=_params(("arbitrary",)),
        name="merge",
    )(x, py, at, *consts)


def _row_tiles_store(ref_view, x):
    rows = x.shape[0]
    for s in range(x.shape[1] // LANES):
        ref_view[pl.ds(s, rows, stride=x.shape[1] // LANES), :] = x[:, s * LANES:(s + 1) * LANES]


def _row_tiles_load(ref_view, rows, width):
    n = width // LANES
    return jnp.concatenate([ref_view[pl.ds(s, rows, stride=n), :] for s in range(n)], axis=1)


def _dispatch_kernel(pos_ref, x1_ref, g_ref, xs_ref, buf_ref, sem_ref):
    i = pl.program_id(0)
    n = pl.num_programs(0)
    td, D = x1_ref.shape
    rt = D // LANES
    slot = i % 2

    def drain(s):
        for _ in range(2):
            pltpu.make_async_copy(buf_ref.at[s], xs_ref.at[pl.ds(0, td * rt)], sem_ref.at[s]).wait()

    @pl.when(i >= 2)
    def _():
        drain(slot)

    _row_tiles_store(buf_ref.at[slot], _rms(x1_ref[...], g_ref[...]))

    def issue(r8, _):
        for u in range(ISSUE_UNROLL):
            r = r8 * ISSUE_UNROLL + u
            for c in range(2):
                dst = pl.multiple_of(pos_ref[0, c * td + r] * rt, rt)
                pltpu.make_async_copy(buf_ref.at[slot, pl.ds(r * rt, rt)], xs_ref.at[pl.ds(dst, rt)],
                                      sem_ref.at[slot]).start(priority=c)
        return 0

    lax.fori_loop(0, td // ISSUE_UNROLL, issue, 0)

    @pl.when(i == n - 1)
    def _():
        drain(slot)

        @pl.when(n >= 2)
        def _():
            drain(1 - slot)


def _dispatch(x1, pos, gffn):
    N, D = x1.shape
    td = min(N, 512)
    assert N % td == 0 and td % ISSUE_UNROLL == 0
    nt = N // td
    rt = D // LANES
    pos3 = pos.reshape(2, nt, td).transpose(1, 0, 2).reshape(nt, 1, 2 * td)
    return pl.pallas_call(
        _dispatch_kernel,
        grid=(nt,),
        in_specs=[pl.BlockSpec((None, 1, 2 * td), lambda i: (i, 0, 0), memory_space=pltpu.SMEM),
                  pl.BlockSpec((td, D), lambda i: (i, 0)), _const_spec((1, D))],
        out_specs=pl.BlockSpec(memory_space=pl.ANY),
        out_shape=jax.ShapeDtypeStruct((2 * N * rt, LANES), F32),
        scratch_shapes=[pltpu.VMEM((2, td * rt, LANES), F32), pltpu.SemaphoreType.DMA((2,))],
        compiler_params=_params(("arbitrary",)),
        name="dispatch",
    )(pos3, x1, gffn)


def _ffn_kernel(tile_ref, ex_ref, lo_ref, hi_ref, first_ref, xs_ref, wg_ref, wu_ref, wd_ref, ys_ref):
    i = pl.program_id(0)
    D = wg_ref.shape[0]
    rt = D // LANES
    tf = xs_ref.shape[0] // rt
    lo = lo_ref[i]
    hi = hi_ref[i]

    @pl.when(hi > lo)
    def _():
        xb = _row_tiles_load(xs_ref, tf, D).astype(BF16)
        gate = jnp.dot(xb, wg_ref[...], preferred_element_type=F32)
        up = jnp.dot(xb, wu_ref[...], preferred_element_type=F32)
        hid = (jax.nn.silu(gate) * up).astype(BF16)
        y = jnp.dot(hid, wd_ref[...], preferred_element_type=F32)
        rows = tile_ref[i] * tf + lax.broadcasted_iota(jnp.int32, (tf, 1), 0)
        mine = (rows >= lo) & (rows < hi)

        @pl.when(first_ref[i] == 1)
        def _():
            _row_tiles_store(ys_ref, jnp.where(mine, y, 0.0))

        @pl.when(first_ref[i] == 0)
        def _():
            _row_tiles_store(ys_ref, jnp.where(mine, y, _row_tiles_load(ys_ref, tf, D)))


def _ffn(xs, sched, w_gate, w_up, w_down, tf):
    D, Hd = w_gate.shape[1:]
    rt = D // LANES
    n_items = sched[0].shape[0]
    rows = pl.BlockSpec((tf * rt, LANES), lambda i, tile, ex, lo, hi, first: (tile[i], 0))
    wspec = lambda a, b_: pl.BlockSpec((None, a, b_), lambda i, tile, ex, lo, hi, first: (ex[i], 0, 0))
    return pl.pallas_call(
        _ffn_kernel,
        grid_spec=pltpu.PrefetchScalarGridSpec(
            num_scalar_prefetch=5,
            grid=(n_items,),
            in_specs=[rows, wspec(D, Hd), wspec(D, Hd), wspec(Hd, D)],
            out_specs=rows,
        ),
        out_shape=jax.ShapeDtypeStruct(xs.shape, F32),
        compiler_params=_params(("arbitrary",)),
        name="expert_ffn",
    )(*sched, xs, w_gate, w_up, w_down)


def _schedule(counts, M, tf):
    n_tiles = M // tf
    n_items = n_tiles + N_EXPERTS - 1
    ends = jnp.cumsum(counts)
    starts = ends - counts
    first_tile = starts // tf
    last_tile = jnp.maximum(ends - 1, 0) // tf
    per = jnp.where(counts > 0, last_tile - first_tile + 1, 0)
    item_end = jnp.cumsum(per)
    item_start = item_end - per
    total = item_end[-1]
    idx = jnp.arange(n_items, dtype=jnp.int32)
    valid = idx < total
    idx_c = jnp.minimum(idx, total - 1)
    ex = jnp.minimum(jnp.sum(item_end[None, :] <= idx_c[:, None], axis=1), N_EXPERTS - 1).astype(jnp.int32)
    onehot = ex[:, None] == jnp.arange(N_EXPERTS)[None, :]
    pick = lambda table: jnp.sum(jnp.where(onehot, table[None, :], 0), axis=1)
    tile = (pick(first_tile) + idx_c - pick(item_start)).astype(jnp.int32)
    lo = jnp.where(valid, pick(starts), 0).astype(jnp.int32)
    hi = jnp.where(valid, pick(ends), 0).astype(jnp.int32)
    prev = jnp.concatenate([jnp.full((1,), -1, jnp.int32), tile[:-1]])
    first = (valid & (tile != prev)).astype(jnp.int32)
    return (tile, ex, lo, hi, first), starts


def _combine_kernel(pos_ref, nxt_ref, x1_ref, wc_ref, g_ref, ys_ref, o_ref, buf_ref, sem_ref):
    i = pl.program_id(0)
    n = pl.num_programs(0)
    tc, D = x1_ref.shape
    rt = D // LANES
    slot = i % 2

    def gather(idx_ref, s):
        def issue(r8, _):
            for u in range(ISSUE_UNROLL):
                r = r8 * ISSUE_UNROLL + u
                for c in range(2):
                    src = pl.multiple_of(idx_ref[0, c * tc + r] * rt, rt)
                    pltpu.make_async_copy(ys_ref.at[pl.ds(src, rt)], buf_ref.at[s, c, pl.ds(r * rt, rt)],
                                          sem_ref.at[s]).start(priority=c)
            return 0
        lax.fori_loop(0, tc // ISSUE_UNROLL, issue, 0)

    @pl.when(i == 0)
    def _():
        gather(pos_ref, slot)

    @pl.when(i + 1 < n)
    def _():
        gather(nxt_ref, 1 - slot)

    for c in range(2):
        pltpu.make_async_copy(ys_ref.at[pl.ds(0, tc * rt)], buf_ref.at[slot, c], sem_ref.at[slot]).wait()

    wc = wc_ref[...]
    r0 = _row_tiles_load(buf_ref.at[slot, 0], tc, D)
    r1 = _row_tiles_load(buf_ref.at[slot, 1], tc, D)
    y = x1_ref[...] + (wc[:, 0:1] * r0 + wc[:, 1:2] * r1)
    o_ref[...] = _rms(y, g_ref[...])


def _combine(x1, pos, wcol, ys, final_g):
    N, D = x1.shape
    tc = min(N, 256)
    assert N % tc == 0 and tc % ISSUE_UNROLL == 0
    nt = N // tc
    rt = D // LANES
    pos3 = pos.reshape(2, nt, tc).transpose(1, 0, 2).reshape(nt, 1, 2 * tc)
    smem = lambda f: pl.BlockSpec((None, 1, 2 * tc), f, memory_space=pltpu.SMEM)
    return pl.pallas_call(
        _combine_kernel,
        grid=(nt,),
        in_specs=[smem(lambda i: (i, 0, 0)), smem(lambda i: (jnp.minimum(i + 1, nt - 1), 0, 0)),
                  pl.BlockSpec((tc, D), lambda i: (i, 0)), pl.BlockSpec((tc, LANES), lambda i: (i, 0)),
                  _const_spec((1, D)), pl.BlockSpec(memory_space=pl.ANY)],
        out_specs=pl.BlockSpec((tc, D), lambda i: (i, 0)),
        out_shape=jax.ShapeDtypeStruct((N, D), F32),
        scratch_shapes=[pltpu.VMEM((2, 2, tc * rt, LANES), F32), pltpu.SemaphoreType.DMA((2,))],
        compiler_params=_params(("arbitrary",)),
        name="combine",
    )(pos3, pos3, x1, wcol, final_g, ys)


def _moe_and_norm(x1, ri, wcol, counts, p):
    N, D = x1.shape
    tf = min(2 * N, 512)
    sched, starts = _schedule(counts, 2 * N, tf)
    hit = ri[0:2][:, None, :] == jnp.arange(N_EXPERTS, dtype=jnp.int32)[None, :, None]
    pos = jnp.sum(jnp.where(hit, starts.astype(jnp.int32)[None, :, None], 0), axis=1) + ri[2:4]
    xs = _dispatch(x1, pos, p["gffn"])
    ys = _ffn(xs, sched, p["w_eg"], p["w_eu"], p["w_ed"], tf)
    return _combine(x1, pos, wcol, ys, p["final_g"])


def _stream(x, pos0, pool_hdr, cache, p, n_heads):
    B, T, D = x.shape
    q, k, v, py, npool = _inproj(x, pool_hdr, pos0, p["gmix"], p["w_a"], p["pool_w"], p["pool_scale"], n_heads)
    if cache is None:
        at = _attn_prompt(q, k, v, p["lam_vecs"], p["subln_g"], n_heads)
    else:
        at = _attn_sample(q, k, v, cache[0], cache[1], p["lam_vecs"], p["subln_g"], n_heads)
    N = B * T
    x1, ri, wcol, cnt = _merge(x.reshape(N, D), py.reshape(N, -1), at.reshape(N, -1), p)
    y = _moe_and_norm(x1, ri, wcol, cnt[:, 0], p)
    return (y.reshape(B, T, D), k.reshape(1, B, T, n_heads, 2 * HEAD_DIM), v.reshape(1, B, T, n_heads, V_DIM),
            npool[None, :, 1:, :])


def kernel(x_prompt, x_sample, cache_k, cache_v, state_pool, norm_mix_g, w_in, b_gate, lambda_q1, lambda_k1,
           lambda_q2, lambda_k2, subln_g, pool_w, pool_scale, w_pool_branch, w_attn_branch, w_out, norm_ffn_g,
           w_router_group, b_router_group, w_router_expert, b_router_expert, w_expert_gate, w_expert_up,
           w_expert_down, final_norm_g):
    assert w_in.shape[0] == 1, "single layer"
    D = x_prompt.shape[-1]
    n_heads = cache_k.shape[3]
    P = pool_scale.shape[-1]
    past = cache_k.shape[2]
    n_a = P + 2 * n_heads * 2 * HEAD_DIM + n_heads * V_DIM

    wr = jnp.concatenate([w_router_group[0], jnp.transpose(w_router_expert[0], (1, 0, 2)).reshape(D, N_EXPERTS)],
                         axis=1)
    br = jnp.concatenate([b_router_group[0], b_router_expert[0].reshape(N_EXPERTS)])
    n_r = wr.shape[1]
    p = dict(
        gmix=norm_mix_g[0].reshape(1, D),
        w_a=w_in[0, :, :n_a].astype(BF16),
        w_gate=w_in[0, :, n_a:].astype(BF16),
        b_gate=b_gate[0].reshape(1, 2 * D),
        lam_vecs=jnp.stack([lambda_q1[0], lambda_k1[0], lambda_q2[0], lambda_k2[0]]),
        subln_g=subln_g[0],
        pool_w=pool_w[0].astype(BF16),
        pool_scale=pool_scale[0],
        w_pb=w_pool_branch[0].astype(BF16),
        w_ab=w_attn_branch[0].astype(BF16),
        w_out=w_out[0].astype(BF16),
        gffn=norm_ffn_g[0].reshape(1, D),
        w_router=jnp.pad(wr, ((0, 0), (0, LANES - n_r))).astype(BF16),
        b_router=jnp.pad(br, (0, LANES - n_r)).reshape(1, LANES),
        w_eg=w_expert_gate[0].astype(BF16),
        w_eu=w_expert_up[0].astype(BF16),
        w_ed=w_expert_down[0].astype(BF16),
        final_g=final_norm_g.reshape(1, D),
    )

    Bp = x_prompt.shape[0]
    Bs = x_sample.shape[0]
    hdr_p = jnp.zeros((Bp, POOL_HDR, P), F32)
    hdr_s = jnp.pad(state_pool[0], ((0, 0), (1, 0), (0, 0)))
    cache = (cache_k[0].reshape(Bs, past * n_heads, -1), cache_v[0].reshape(Bs, past * n_heads, -1))

    yp, kp, vp, pp = _stream(x_prompt, 0, hdr_p, None, p, n_heads)
    ys, ks, vs, ps = _stream(x_sample, past, hdr_s, cache, p, n_heads)
    return (yp, ys, kp, vp, pp, ks, vs, ps)
```

```python
import functools
import math

import jax
import jax.numpy as jnp
from jax import lax
from jax.experimental import pallas as pl
from jax.experimental.pallas import tpu as pltpu

CHUNK = 64
HEAD_DIM = 64
V_DIM = 2 * HEAD_DIM
ROT_DIM = HEAD_DIM // 4
ROPE_THETA = 500000.0
ATTN_SCALE = HEAD_DIM ** -0.5
Q_SCALE = ATTN_SCALE * math.log2(math.e)
POOL_WINDOWS = (2, 4, 8, 16)
POOL_GROUP_DIM = 128
POOL_BUF = max(POOL_WINDOWS) - 1
POOL_HDR = POOL_BUF + 1
N_EXPERT_GROUPS = 4
EXPERTS_PER_GROUP = 4
N_EXPERTS = N_EXPERT_GROUPS * EXPERTS_PER_GROUP
RMS_EPS = 1e-6
LAM_INIT = 0.8 - 0.6 * math.exp(-0.3 * 0)

LANES = 128
ISSUE_UNROLL = 8
ONES_ROWS = 16
FFN_ROW_CHUNKS = 2
VMEM_LIMIT = 56 * 1024 * 1024
NEG_BIG = -1e30

F32 = jnp.float32
BF16 = jnp.bfloat16


def _params(semantics):
    return pltpu.CompilerParams(dimension_semantics=semantics, vmem_limit_bytes=VMEM_LIMIT)


def _const_spec(shape):
    nd = len(shape)
    return pl.BlockSpec(shape, lambda *_: (0,) * nd)


def _rms(x, g):
    ms = jnp.mean(x * x, axis=-1, keepdims=True)
    return (x * lax.rsqrt(ms + RMS_EPS)) * g


def _inproj_kernel(x_ref, g_ref, w_ref, cos_ref, sa_ref, sb_ref, pbuf_ref, pw_ref, ps_ref,
                   q_ref, k_ref, v_ref, py_ref, npool_ref, ext_ref, carry_ref, *, pos0, n_heads):
    t = pl.program_id(0)
    b = pl.program_id(1)
    tm = x_ref.shape[0]
    pool_w = ext_ref.shape[1]
    qk_w = n_heads * 2 * HEAD_DIM

    h = _rms(x_ref[...], g_ref[...]).astype(BF16)

    @pl.when(t == 0)
    def _():
        ext_ref[0:POOL_HDR, :] = pbuf_ref[...]

    @pl.when(t > 0)
    def _():
        ext_ref[0:POOL_HDR, :] = carry_ref[b]

    ext_ref[POOL_HDR:, :] = jnp.dot(h, w_ref[:, 0:pool_w], preferred_element_type=F32)
    tail = ext_ref[tm:tm + POOL_HDR, :]
    carry_ref[b] = tail
    npool_ref[...] = tail

    cos = cos_ref[...]
    sa = sa_ref[...]
    sb = sb_ref[...]
    col_chunk = 4 * LANES
    for which, out_ref in ((0, q_ref), (1, k_ref)):
        for c in range(0, qk_w, col_chunk):
            base = pool_w + which * qk_w + c
            z = jnp.dot(h, w_ref[:, base:base + col_chunk], preferred_element_type=F32)
            for j in range(0, col_chunk, LANES):
                zc = z[:, j:j + LANES]
                rot = (zc * cos + pltpu.roll(zc, LANES - ROT_DIM // 2, 1) * sa
                       + pltpu.roll(zc, ROT_DIM // 2, 1) * sb)
                if which == 0:
                    out_ref[:, c + j:c + j + LANES] = (rot * Q_SCALE).astype(BF16)
                else:
                    out_ref[:, c + j:c + j + LANES] = rot

    v_w = v_ref.shape[1]
    for c in range(0, v_w, col_chunk):
        base = pool_w + 2 * qk_w + c
        v_ref[:, c:c + col_chunk] = jnp.dot(h, w_ref[:, base:base + col_chunk], preferred_element_type=F32)

    pos = pos0 + t * tm + lax.broadcasted_iota(jnp.int32, (tm, 1), 0)
    for g, w in enumerate(POOL_WINDOWS):
        sl = slice(g * POOL_GROUP_DIM, (g + 1) * POOL_GROUP_DIM)
        u = ext_ref[POOL_HDR:POOL_HDR + tm, sl]
        ws = u
        for j in range(1, w):
            ws = ws + ext_ref[POOL_HDR - j:POOL_HDR - j + tm, sl]
        cnt = jnp.minimum(w, pos + 1).astype(F32)
        d = (ws / cnt - u).astype(BF16)
        y = jnp.dot(d, pw_ref[g], preferred_element_type=F32) * ps_ref[:, sl]
        py_ref[:, sl] = y.astype(BF16)


def _inproj(x, pool_hdr, pos0, norm_g, w_a, pool_w, pool_scale, n_heads):
    B, T, D = x.shape
    P = pool_scale.shape[-1]
    qk_w = n_heads * 2 * HEAD_DIM
    v_w = n_heads * V_DIM
    tm = min(T, 512)
    assert T % tm == 0 and T >= POOL_HDR and tm % 16 == 0
    nt = T // tm

    pos = pos0 + jnp.arange(T)
    inv_freq = jnp.power(ROPE_THETA, -jnp.arange(0, ROT_DIM, 2, dtype=F32) / ROT_DIM)
    ang = pos.astype(F32)[:, None] * inv_freq[None, :]
    cos, sin = jnp.cos(ang), jnp.sin(ang)
    half = ROT_DIM // 2
    ones = jnp.ones((T, HEAD_DIM - ROT_DIM), F32)
    zeros_h = jnp.zeros((T, half), F32)
    zeros_r = jnp.zeros((T, HEAD_DIM - ROT_DIM), F32)
    cos_t = jnp.tile(jnp.concatenate([cos, cos, ones], axis=1), (1, LANES // HEAD_DIM))
    sa_t = jnp.tile(jnp.concatenate([-sin, zeros_h, zeros_r], axis=1), (1, LANES // HEAD_DIM))
    sb_t = jnp.tile(jnp.concatenate([zeros_h, sin, zeros_r], axis=1), (1, LANES // HEAD_DIM))

    row = lambda w: pl.BlockSpec((None, tm, w), lambda t, b: (b, t, 0))
    tab = pl.BlockSpec((tm, LANES), lambda t, b: (t, 0))
    hdr = pl.BlockSpec((None, POOL_HDR, P), lambda t, b: (b, 0, 0))
    hdr_out = pl.BlockSpec((None, POOL_HDR, P), lambda t, b: (jnp.where(t == nt - 1, b, 0), 0, 0))
    kern = functools.partial(_inproj_kernel, pos0=pos0, n_heads=n_heads)
    return pl.pallas_call(
        kern,
        grid=(nt, B),
        in_specs=[row(D), _const_spec((1, D)), _const_spec(w_a.shape), tab, tab, tab, hdr,
                  _const_spec(pool_w.shape), _const_spec((1, P))],
        out_specs=[row(qk_w), row(qk_w), row(v_w), row(P), hdr_out],
        out_shape=[jax.ShapeDtypeStruct((B, T, qk_w), BF16),
                   jax.ShapeDtypeStruct((B, T, qk_w), F32),
                   jax.ShapeDtypeStruct((B, T, v_w), F32),
                   jax.ShapeDtypeStruct((B, T, P), BF16),
                   jax.ShapeDtypeStruct((B, POOL_HDR, P), F32)],
        scratch_shapes=[pltpu.VMEM((POOL_HDR + tm, P), F32), pltpu.VMEM((B, POOL_HDR, P), F32)],
        compiler_params=_params(("arbitrary", "arbitrary")),
        name="inproj",
    )(x, norm_g.reshape(1, D), w_a, cos_t, sa_t, sb_t, pool_hdr, pool_w, pool_scale.reshape(1, P))


def _lambda(lam_ref):
    lv = lam_ref[...]
    a = jnp.sum(lv[0:1] * lv[1:2], axis=-1, keepdims=True)
    c = jnp.sum(lv[2:3] * lv[3:4], axis=-1, keepdims=True)
    return jnp.exp(a) - jnp.exp(c) + LAM_INIT


def _split_maps(q):
    lane = lax.broadcasted_iota(jnp.int32, q.shape, 1)
    zero = jnp.zeros_like(q)
    return jnp.where(lane < HEAD_DIM, q, zero), jnp.where(lane >= HEAD_DIM, q, zero)


def _nt_dot(a, b):
    return lax.dot_general(a, b, (((1,), (1,)), ((), ())), preferred_element_type=F32)


def _head_out(o, g):
    ms = jnp.mean(o * o, axis=-1, keepdims=True)
    return ((o * lax.rsqrt(ms + RMS_EPS)) * g) * (1.0 - LAM_INIT)


def _attn_prompt_kernel(q_ref, k_ref, v_ref, lam_ref, g_ref, o_ref,
                        kb_ref, vt_ref, acc1_ref, acc2_ref, stat_ref, sa_ref, sb_ref, sc_ref, *, tq):
    T = q_ref.shape[0]
    nq = T // tq

    kb_ref[...] = k_ref[...].astype(BF16)
    ones_row = jnp.where(lax.broadcasted_iota(jnp.int32, (ONES_ROWS, tq), 0) == 0, 1.0, 0.0).astype(BF16)
    for c in range(nq):
        vt_ref[c, 0:V_DIM, :] = v_ref[c * tq:(c + 1) * tq, :].T.astype(BF16)
        vt_ref[c, V_DIM:, :] = ones_row

    def q_maps(i):
        qt = q_ref[pl.ds(pl.multiple_of(i * tq, tq), tq), :].astype(F32).T.astype(BF16)
        row = lax.broadcasted_iota(jnp.int32, qt.shape, 0)
        zero = jnp.zeros_like(qt)
        return jnp.where(row < HEAD_DIM, qt, zero), jnp.where(row >= HEAD_DIM, qt, zero)

    def scores(j, s_ref, qm):
        kj = kb_ref[pl.ds(pl.multiple_of(j * tq, tq), tq), :]
        s_ref[0] = jnp.dot(kj, qm[0], preferred_element_type=F32)
        s_ref[1] = jnp.dot(kj, qm[1], preferred_element_type=F32)

    def softmax_pv(j, s_ref, diagonal):
        vtj = vt_ref[j]
        if diagonal:
            key_chunk = lax.broadcasted_iota(jnp.int32, (tq, tq), 0) // CHUNK
            qry_chunk = lax.broadcasted_iota(jnp.int32, (tq, tq), 1) // CHUNK
            mask = key_chunk <= qry_chunk
        for c, acc_ref in ((0, acc1_ref), (1, acc2_ref)):
            s = s_ref[c]
            if diagonal:
                s = jnp.where(mask, s, NEG_BIG)
            m = stat_ref[c:c + 1, :]
            mn = jnp.maximum(m, jnp.max(s, axis=0, keepdims=True))
            alpha = jnp.exp2(m - mn)
            p = jnp.exp2(s - mn)
            stat_ref[c:c + 1, :] = mn
            acc_ref[...] = alpha * acc_ref[...] + jnp.dot(vtj, p.astype(BF16), preferred_element_type=F32)

    lam = _lambda(lam_ref)
    scores(0, sc_ref, q_maps(0))

    def q_tile(i, _):
        qm = q_maps(i)
        qn = q_maps(jnp.minimum(i + 1, nq - 1))
        acc1_ref[...] = jnp.zeros_like(acc1_ref)
        acc2_ref[...] = jnp.zeros_like(acc2_ref)
        stat_ref[...] = jnp.full(stat_ref.shape, NEG_BIG, F32)

        @pl.when(i == 0)
        def _():
            softmax_pv(0, sc_ref, True)
            scores(0, sc_ref, qn)

        @pl.when(i > 0)
        def _():
            scores(1, sa_ref, qm)
            softmax_pv(0, sc_ref, False)

        def pair(p, _):
            t = 2 * p + 1
            scores(t + 1, sb_ref, qm)
            softmax_pv(t, sa_ref, False)
            scores(t + 2, sa_ref, qm)
            softmax_pv(t + 1, sb_ref, False)
            return 0

        lax.fori_loop(0, jnp.maximum(i - 1, 0) // 2, pair, 0)

        @pl.when(i % 2 == 1)
        def _():
            scores(0, sc_ref, qn)
            softmax_pv(i, sa_ref, True)

        @pl.when((i % 2 == 0) & (i >= 2))
        def _():
            scores(i, sb_ref, qm)
            softmax_pv(i - 1, sa_ref, False)
            scores(0, sc_ref, qn)
            softmax_pv(i, sb_ref, True)

        l1 = acc1_ref[V_DIM:V_DIM + 1, :]
        l2 = acc2_ref[V_DIM:V_DIM + 1, :]
        o = acc1_ref[0:V_DIM, :] / l1 - lam * (acc2_ref[0:V_DIM, :] / l2)
        ms = jnp.mean(o * o, axis=0, keepdims=True)
        o = ((o * lax.rsqrt(ms + RMS_EPS)) * g_ref[...]) * (1.0 - LAM_INIT)
        o_ref[pl.ds(pl.multiple_of(i * tq, tq), tq), :] = o.T.astype(BF16)
        return 0

    lax.fori_loop(0, nq, q_tile, 0)


def _attn_prompt(q, k, v, lam_vecs, subln_g, n_heads):
    B, T, _ = q.shape
    tq = min(T, 512)
    assert T % tq == 0 and tq % LANES == 0
    blk = pl.BlockSpec((None, T, V_DIM), lambda b, h: (b, 0, h))
    score_buf = pltpu.VMEM((2, tq, tq), F32)
    return pl.pallas_call(
        functools.partial(_attn_prompt_kernel, tq=tq),
        grid=(B, n_heads),
        in_specs=[blk, blk, blk, _const_spec(lam_vecs.shape), _const_spec((V_DIM, 1))],
        out_specs=blk,
        out_shape=jax.ShapeDtypeStruct((B, T, n_heads * V_DIM), BF16),
        scratch_shapes=[pltpu.VMEM((T, V_DIM), BF16), pltpu.VMEM((T // tq, V_DIM + ONES_ROWS, tq), BF16),
                        pltpu.VMEM((V_DIM + ONES_ROWS, tq), F32), pltpu.VMEM((V_DIM + ONES_ROWS, tq), F32),
                        pltpu.VMEM((2, tq), F32), score_buf, score_buf, score_buf],
        compiler_params=_params(("arbitrary", "arbitrary")),
        name="attn_prompt",
    )(q, k, v, lam_vecs, subln_g.reshape(V_DIM, 1))


def _attn_sample_kernel(q_ref, ck_ref, cv_ref, nk_ref, nv_ref, lam_ref, g_ref, o_ref, *, past, n_heads):
    S = q_ref.shape[0]
    P = past
    q_chunk = (past + lax.broadcasted_iota(jnp.int32, (S, 1), 0)) // CHUNK
    mask_c = (lax.broadcasted_iota(jnp.int32, (S, P), 1) // CHUNK) <= q_chunk
    mask_n = ((past + lax.broadcasted_iota(jnp.int32, (S, S), 1)) // CHUNK) <= q_chunk
    lam = _lambda(lam_ref)
    for h in range(n_heads):
        cols = slice(h * V_DIM, (h + 1) * V_DIM)
        q1, q2 = _split_maps(q_ref[:, cols])
        ck = ck_ref[pl.ds(h, P, stride=n_heads), :].astype(BF16)
        cv = cv_ref[pl.ds(h, P, stride=n_heads), :].astype(BF16)
        nk = nk_ref[:, cols].astype(BF16)
        nv = nv_ref[:, cols].astype(BF16)
        outs = []
        for qm in (q1, q2):
            sc = jnp.where(mask_c, _nt_dot(qm, ck), NEG_BIG)
            sn = jnp.where(mask_n, _nt_dot(qm, nk), NEG_BIG)
            m = jnp.maximum(jnp.max(sc, axis=-1, keepdims=True), jnp.max(sn, axis=-1, keepdims=True))
            pc = jnp.exp2(sc - m)
            pn = jnp.exp2(sn - m)
            l = jnp.sum(pc, axis=-1, keepdims=True) + jnp.sum(pn, axis=-1, keepdims=True)
            acc = (jnp.dot(pc.astype(BF16), cv, preferred_element_type=F32)
                   + jnp.dot(pn.astype(BF16), nv, preferred_element_type=F32))
            outs.append(acc / l)
        o = outs[0] - lam * outs[1]
        o_ref[:, cols] = _head_out(o, g_ref[...]).astype(BF16)


def _attn_sample(q, k, v, cache_k, cache_v, lam_vecs, subln_g, n_heads):
    B, S, W = q.shape
    P = cache_k.shape[1] // n_heads
    nblk = pl.BlockSpec((None, S, W), lambda b: (b, 0, 0))
    cblk = pl.BlockSpec((None, P * n_heads, V_DIM), lambda b: (b, 0, 0))
    kern = functools.partial(_attn_sample_kernel, past=P, n_heads=n_heads)
    return pl.pallas_call(
        kern,
        grid=(B,),
        in_specs=[nblk, cblk, cblk, nblk, nblk, _const_spec(lam_vecs.shape), _const_spec((1, V_DIM))],
        out_specs=nblk,
        out_shape=jax.ShapeDtypeStruct((B, S, W), BF16),
        compiler_params=_params(("arbitrary",)),
        name="attn_sample",
    )(q, cache_k, cache_v, k, v, lam_vecs, subln_g.reshape(1, V_DIM))


def _merge_kernel(x_ref, py_ref, at_ref, gmix_ref, wg_ref, bg_ref, wpb_ref, wab_ref, wo_ref,
                  gffn_ref, wr_ref, br_ref, tri_ref,
                  x1_ref, ri_ref, wc_ref, cnt_ref, wrow_ref, run_ref):
    step = pl.program_id(0)
    tm, D = x_ref.shape

    @pl.when(step == 0)
    def _():
        wrow_ref[...] = jnp.zeros_like(wrow_ref)
        run_ref[...] = jnp.zeros_like(run_ref)

    x = x_ref[...]
    h = _rms(x, gmix_ref[...]).astype(BF16)
    gl = jnp.dot(h, wg_ref[...], preferred_element_type=F32) + bg_ref[...]
    a = jnp.dot(py_ref[...], wpb_ref[...], preferred_element_type=F32)
    bb = jnp.dot(at_ref[...], wab_ref[...], preferred_element_type=F32)
    merged = jax.nn.sigmoid(gl[:, :D]) * a + jax.nn.sigmoid(gl[:, D:]) * bb
    x1 = x + jnp.dot(merged.astype(BF16), wo_ref[...], preferred_element_type=F32)
    x1_ref[...] = x1

    h2 = _rms(x1, gffn_ref[...]).astype(BF16)
    logits = jnp.dot(h2, wr_ref[...], preferred_element_type=F32) + br_ref[...]
    lt = logits.T
    G = N_EXPERT_GROUPS
    E = EXPERTS_PER_GROUP
    best = lt[0:1]
    gsel = jnp.zeros((1, tm), jnp.int32)
    for g in range(1, G):
        better = lt[g:g + 1] > best
        gsel = jnp.where(better, g, gsel)
        best = jnp.where(better, lt[g:g + 1], best)
    gden = jnp.zeros((1, tm), F32)
    for g in range(G):
        gden = gden + jnp.exp(lt[g:g + 1] - best)
    g_w = 1.0 / gden
    el = [lt[G + e:G + e + 1] for e in range(E)]
    for g in range(1, G):
        sel = gsel == g
        el = [jnp.where(sel, lt[G + g * E + e:G + g * E + e + 1], el[e]) for e in range(E)]
    v0 = el[0]
    i0 = jnp.zeros((1, tm), jnp.int32)
    for e in range(1, E):
        better = el[e] > v0
        i0 = jnp.where(better, e, i0)
        v0 = jnp.where(better, el[e], v0)
    v1 = jnp.full((1, tm), -jnp.inf, F32)
    i1 = jnp.zeros((1, tm), jnp.int32)
    for e in range(E):
        better = (i0 != e) & (el[e] > v1)
        i1 = jnp.where(better, e, i1)
        v1 = jnp.where(better, el[e], v1)
    ex = jnp.exp(v1 - v0)
    w0 = (1.0 / (1.0 + ex)) * g_w
    w1 = (ex / (1.0 + ex)) * g_w
    e0 = gsel * E + i0
    e1 = gsel * E + i1

    eid = lax.broadcasted_iota(jnp.int32, (N_EXPERTS, tm), 0)
    oh0 = eid == e0
    oh1 = eid == e1
    oh = jnp.where(oh0 | oh1, 1.0, 0.0).astype(BF16)
    pref = jnp.dot(oh, tri_ref[...], preferred_element_type=F32)
    rank = run_ref[...] + pref - 1.0
    r0 = jnp.sum(jnp.where(oh0, rank, 0.0), axis=0, keepdims=True)
    r1 = jnp.sum(jnp.where(oh1, rank, 0.0), axis=0, keepdims=True)
    run_ref[...] = run_ref[...] + pref[:, tm - 1:tm]
    cnt_ref[...] = jnp.broadcast_to(run_ref[...], cnt_ref.shape).astype(jnp.int32)

    ri_ref[0:1, :] = e0
    ri_ref[1:2, :] = e1
    ri_ref[2:3, :] = r0.astype(jnp.int32)
    ri_ref[3:4, :] = r1.astype(jnp.int32)
    wrow_ref[0:1, :] = w0
    wrow_ref[1:2, :] = w1
    wc_ref[...] = wrow_ref[...].T


def _merge(x, py, at, p):
    N, D = x.shape
    tm = min(N, 512)
    assert N % tm == 0
    tri = (jnp.arange(tm)[:, None] <= jnp.arange(tm)[None, :]).astype(BF16)
    row = lambda w: pl.BlockSpec((tm, w), lambda i: (i, 0))
    consts = [p["gmix"], p["w_gate"], p["b_gate"], p["w_pb"], p["w_ab"], p["w_out"],
              p["gffn"], p["w_router"], p["b_router"], tri]
    return pl.pallas_call(
        _merge_kernel,
        grid=(N // tm,),
        in_specs=[row(D), row(py.shape[1]), row(at.shape[1])] + [_const_spec(c.shape) for c in consts],
        out_specs=[row(D), pl.BlockSpec((4, tm), lambda i: (0, i)), row(LANES),
                   _const_spec((N_EXPERTS, LANES))],
        out_shape=[jax.ShapeDtypeStruct((N, D), F32),
                   jax.ShapeDtypeStruct((4, N), jnp.int32),
                   jax.ShapeDtypeStruct((N, LANES), F32),
                   jax.ShapeDtypeStruct((N_EXPERTS, LANES), jnp.int32)],
        scratch_shapes=[pltpu.VMEM((LANES, tm), F32), pltpu.VMEM((N_EXPERTS, 1), F32)],
        compiler_params=_params(("arbitrary",)),
        name="merge",
    )(x, py, at, *consts)


def _row_tiles_store(ref_view, x, row0=0):
    rows = x.shape[0]
    n = x.shape[1] // LANES
    for s in range(n):
        ref_view[pl.ds(row0 * n + s, rows, stride=n), :] = x[:, s * LANES:(s + 1) * LANES]


def _row_tiles_load(ref_view, rows, width, row0=0):
    n = width // LANES
    return jnp.concatenate([ref_view[pl.ds(row0 * n + s, rows, stride=n), :] for s in range(n)], axis=1)


def _dispatch_kernel(pos_ref, x1_ref, g_ref, xs_ref, buf_ref, sem_ref):
    i = pl.program_id(0)
    n = pl.num_programs(0)
    td, D = x1_ref.shape
    rt = D // LANES
    slot = i % 2

    def drain(s):
        for _ in range(2):
            pltpu.make_async_copy(buf_ref.at[s], xs_ref.at[pl.ds(0, td * rt)], sem_ref.at[s]).wait()

    @pl.when(i >= 2)
    def _():
        drain(slot)

    _row_tiles_store(buf_ref.at[slot], _rms(x1_ref[...], g_ref[...]))

    def issue(r8, _):
        for u in range(ISSUE_UNROLL):
            r = r8 * ISSUE_UNROLL + u
            for c in range(2):
                dst = pl.multiple_of(pos_ref[0, c * td + r] * rt, rt)
                pltpu.make_async_copy(buf_ref.at[slot, pl.ds(r * rt, rt)], xs_ref.at[pl.ds(dst, rt)],
                                      sem_ref.at[slot]).start(priority=c)
        return 0

    lax.fori_loop(0, td // ISSUE_UNROLL, issue, 0)

    @pl.when(i == n - 1)
    def _():
        drain(slot)

        @pl.when(n >= 2)
        def _():
            drain(1 - slot)


def _dispatch(x1, pos, gffn):
    N, D = x1.shape
    td = min(N, 512)
    assert N % td == 0 and td % ISSUE_UNROLL == 0
    nt = N // td
    rt = D // LANES
    pos3 = pos.reshape(2, nt, td).transpose(1, 0, 2).reshape(nt, 1, 2 * td)
    return pl.pallas_call(
        _dispatch_kernel,
        grid=(nt,),
        in_specs=[pl.BlockSpec((None, 1, 2 * td), lambda i: (i, 0, 0), memory_space=pltpu.SMEM),
                  pl.BlockSpec((td, D), lambda i: (i, 0)), _const_spec((1, D))],
        out_specs=pl.BlockSpec(memory_space=pl.ANY),
        out_shape=jax.ShapeDtypeStruct((2 * N * rt, LANES), F32),
        scratch_shapes=[pltpu.VMEM((2, td * rt, LANES), F32), pltpu.SemaphoreType.DMA((2,))],
        compiler_params=_params(("arbitrary",)),
        name="dispatch",
    )(pos3, x1, gffn)


def _ffn_kernel(tile_ref, ex_ref, lo_ref, hi_ref, first_ref, xs_ref, wg_ref, wu_ref, wd_ref, ys_ref):
    i = pl.program_id(0)
    D = wg_ref.shape[0]
    rt = D // LANES
    tf = xs_ref.shape[0] // rt
    lo = lo_ref[i]
    hi = hi_ref[i]

    @pl.when(first_ref[i] == 1)
    def _():
        ys_ref[...] = jnp.zeros_like(ys_ref)

    @pl.when(hi > lo)
    def _():
        rc = tf // FFN_ROW_CHUNKS
        for ch in range(FFN_ROW_CHUNKS):
            r0 = ch * rc
            xb = _row_tiles_load(xs_ref, rc, D, r0).astype(BF16)
            gate = jnp.dot(xb, wg_ref[...], preferred_element_type=F32)
            up = jnp.dot(xb, wu_ref[...], preferred_element_type=F32)
            hid = (jax.nn.silu(gate) * up).astype(BF16)
            y = jnp.dot(hid, wd_ref[...], preferred_element_type=F32)
            rows = tile_ref[i] * tf + r0 + lax.broadcasted_iota(jnp.int32, (rc, 1), 0)
            mine = (rows >= lo) & (rows < hi)
            _row_tiles_store(ys_ref, jnp.where(mine, y, _row_tiles_load(ys_ref, rc, D, r0)), r0)


def _ffn(xs, sched, w_gate, w_up, w_down, tf):
    D, Hd = w_gate.shape[1:]
    rt = D // LANES
    n_items = sched[0].shape[0]
    rows = pl.BlockSpec((tf * rt, LANES), lambda i, tile, ex, lo, hi, first: (tile[i], 0))
    wspec = lambda a, b_: pl.BlockSpec((None, a, b_), lambda i, tile, ex, lo, hi, first: (ex[i], 0, 0))
    return pl.pallas_call(
        _ffn_kernel,
        grid_spec=pltpu.PrefetchScalarGridSpec(
            num_scalar_prefetch=5,
            grid=(n_items,),
            in_specs=[rows, wspec(D, Hd), wspec(D, Hd), wspec(Hd, D)],
            out_specs=rows,
        ),
        out_shape=jax.ShapeDtypeStruct(xs.shape, F32),
        compiler_params=_params(("arbitrary",)),
        name="expert_ffn",
    )(*sched, xs, w_gate, w_up, w_down)


def _schedule(counts, M, tf):
    n_tiles = M // tf
    n_items = n_tiles + N_EXPERTS - 1
    ends = jnp.cumsum(counts)
    starts = ends - counts
    first_tile = starts // tf
    last_tile = jnp.maximum(ends - 1, 0) // tf
    per = jnp.where(counts > 0, last_tile - first_tile + 1, 0)
    item_end = jnp.cumsum(per)
    item_start = item_end - per
    total = item_end[-1]
    idx = jnp.arange(n_items, dtype=jnp.int32)
    valid = idx < total
    idx_c = jnp.minimum(idx, total - 1)
    ex = jnp.minimum(jnp.sum(item_end[None, :] <= idx_c[:, None], axis=1), N_EXPERTS - 1).astype(jnp.int32)
    onehot = ex[:, None] == jnp.arange(N_EXPERTS)[None, :]
    pick = lambda table: jnp.sum(jnp.where(onehot, table[None, :], 0), axis=1)
    tile = (pick(first_tile) + idx_c - pick(item_start)).astype(jnp.int32)
    lo = jnp.where(valid, pick(starts), 0).astype(jnp.int32)
    hi = jnp.where(valid, pick(ends), 0).astype(jnp.int32)
    prev = jnp.concatenate([jnp.full((1,), -1, jnp.int32), tile[:-1]])
    first = (valid & (tile != prev)).astype(jnp.int32)
    return (tile, ex, lo, hi, first), starts


def _combine_kernel(pos_ref, nxt_ref, x1_ref, wc_ref, g_ref, ys_ref, o_ref, buf_ref, sem_ref):
    i = pl.program_id(0)
    n = pl.num_programs(0)
    tc, D = x1_ref.shape
    rt = D // LANES
    slot = i % 2

    def gather(idx_ref, s):
        def issue(r8, _):
            for u in range(ISSUE_UNROLL):
                r = r8 * ISSUE_UNROLL + u
                for c in range(2):
                    src = pl.multiple_of(idx_ref[0, c * tc + r] * rt, rt)
                    pltpu.make_async_copy(ys_ref.at[pl.ds(src, rt)], buf_ref.at[s, c, pl.ds(r * rt, rt)],
                                          sem_ref.at[s]).start(priority=c)
            return 0
        lax.fori_loop(0, tc // ISSUE_UNROLL, issue, 0)

    @pl.when(i == 0)
    def _():
        gather(pos_ref, slot)

    @pl.when(i + 1 < n)
    def _():
        gather(nxt_ref, 1 - slot)

    for c in range(2):
        pltpu.make_async_copy(ys_ref.at[pl.ds(0, tc * rt)], buf_ref.at[slot, c], sem_ref.at[slot]).wait()

    wc = wc_ref[...]
    r0 = _row_tiles_load(buf_ref.at[slot, 0], tc, D)
    r1 = _row_tiles_load(buf_ref.at[slot, 1], tc, D)
    y = x1_ref[...] + (wc[:, 0:1] * r0 + wc[:, 1:2] * r1)
    o_ref[...] = _rms(y, g_ref[...])


def _combine(x1, pos, wcol, ys, final_g):
    N, D = x1.shape
    tc = min(N, 256)
    assert N % tc == 0 and tc % ISSUE_UNROLL == 0
    nt = N // tc
    rt = D // LANES
    pos3 = pos.reshape(2, nt, tc).transpose(1, 0, 2).reshape(nt, 1, 2 * tc)
    smem = lambda f: pl.BlockSpec((None, 1, 2 * tc), f, memory_space=pltpu.SMEM)
    return pl.pallas_call(
        _combine_kernel,
        grid=(nt,),
        in_specs=[smem(lambda i: (i, 0, 0)), smem(lambda i: (jnp.minimum(i + 1, nt - 1), 0, 0)),
                  pl.BlockSpec((tc, D), lambda i: (i, 0)), pl.BlockSpec((tc, LANES), lambda i: (i, 0)),
                  _const_spec((1, D)), pl.BlockSpec(memory_space=pl.ANY)],
        out_specs=pl.BlockSpec((tc, D), lambda i: (i, 0)),
        out_shape=jax.ShapeDtypeStruct((N, D), F32),
        scratch_shapes=[pltpu.VMEM((2, 2, tc * rt, LANES), F32), pltpu.SemaphoreType.DMA((2,))],
        compiler_params=_params(("arbitrary",)),
        name="combine",
    )(pos3, pos3, x1, wcol, final_g, ys)


def _moe_and_norm(x1, ri, wcol, counts, p):
    N, D = x1.shape
    tf = min(2 * N, 512)
    sched, starts = _schedule(counts, 2 * N, tf)
    hit = ri[0:2][:, None, :] == jnp.arange(N_EXPERTS, dtype=jnp.int32)[None, :, None]
    pos = jnp.sum(jnp.where(hit, starts.astype(jnp.int32)[None, :, None], 0), axis=1) + ri[2:4]
    xs = _dispatch(x1, pos, p["gffn"])
    ys = _ffn(xs, sched, p["w_eg"], p["w_eu"], p["w_ed"], tf)
    return _combine(x1, pos, wcol, ys, p["final_g"])


def _stream(x, pos0, pool_hdr, cache, p, n_heads):
    B, T, D = x.shape
    q, k, v, py, npool = _inproj(x, pool_hdr, pos0, p["gmix"], p["w_a"], p["pool_w"], p["pool_scale"], n_heads)
    if cache is None:
        at = _attn_prompt(q, k, v, p["lam_vecs"], p["subln_g"], n_heads)
    else:
        at = _attn_sample(q, k, v, cache[0], cache[1], p["lam_vecs"], p["subln_g"], n_heads)
    N = B * T
    x1, ri, wcol, cnt = _merge(x.reshape(N, D), py.reshape(N, -1), at.reshape(N, -1), p)
    y = _moe_and_norm(x1, ri, wcol, cnt[:, 0], p)
    return (y.reshape(B, T, D), k.reshape(1, B, T, n_heads, 2 * HEAD_DIM), v.reshape(1, B, T, n_heads, V_DIM),
            npool[None, :, 1:, :])


def kernel(x_prompt, x_sample, cache_k, cache_v, state_pool, norm_mix_g, w_in, b_gate, lambda_q1, lambda_k1,
           lambda_q2, lambda_k2, subln_g, pool_w, pool_scale, w_pool_branch, w_attn_branch, w_out, norm_ffn_g,
           w_router_group, b_router_group, w_router_expert, b_router_expert, w_expert_gate, w_expert_up,
           w_expert_down, final_norm_g):
    assert w_in.shape[0] == 1, "single layer"
    D = x_prompt.shape[-1]
    n_heads = cache_k.shape[3]
    P = pool_scale.shape[-1]
    past = cache_k.shape[2]
    n_a = P + 2 * n_heads * 2 * HEAD_DIM + n_heads * V_DIM

    wr = jnp.concatenate([w_router_group[0], jnp.transpose(w_router_expert[0], (1, 0, 2)).reshape(D, N_EXPERTS)],
                         axis=1)
    br = jnp.concatenate([b_router_group[0], b_router_expert[0].reshape(N_EXPERTS)])
    n_r = wr.shape[1]
    p = dict(
        gmix=norm_mix_g[0].reshape(1, D),
        w_a=w_in[0, :, :n_a].astype(BF16),
        w_gate=w_in[0, :, n_a:].astype(BF16),
        b_gate=b_gate[0].reshape(1, 2 * D),
        lam_vecs=jnp.stack([lambda_q1[0], lambda_k1[0], lambda_q2[0], lambda_k2[0]]),
        subln_g=subln_g[0],
        pool_w=pool_w[0].astype(BF16),
        pool_scale=pool_scale[0],
        w_pb=w_pool_branch[0].astype(BF16),
        w_ab=w_attn_branch[0].astype(BF16),
        w_out=w_out[0].astype(BF16),
        gffn=norm_ffn_g[0].reshape(1, D),
        w_router=jnp.pad(wr, ((0, 0), (0, LANES - n_r))).astype(BF16),
        b_router=jnp.pad(br, (0, LANES - n_r)).reshape(1, LANES),
        w_eg=w_expert_gate[0].astype(BF16),
        w_eu=w_expert_up[0].astype(BF16),
        w_ed=w_expert_down[0].astype(BF16),
        final_g=final_norm_g.reshape(1, D),
    )

    Bp = x_prompt.shape[0]
    Bs = x_sample.shape[0]
    hdr_p = jnp.zeros((Bp, POOL_HDR, P), F32)
    hdr_s = jnp.pad(state_pool[0], ((0, 0), (1, 0), (0, 0)))
    cache = (cache_k[0].reshape(Bs, past * n_heads, -1), cache_v[0].reshape(Bs, past * n_heads, -1))

    yp, kp, vp, pp = _stream(x_prompt, 0, hdr_p, None, p, n_heads)
    ys, ks, vs, ps = _stream(x_sample, past, hdr_s, cache, p, n_heads)
    return (yp, ys, kp, vp, pp, ks, vs, ps)
```

```python
import functools
import math

import jax
import jax.numpy as jnp
from jax import lax
from jax.experimental import pallas as pl
from jax.experimental.pallas import tpu as pltpu

CHUNK = 64
HEAD_DIM = 64
V_DIM = 2 * HEAD_DIM
ROT_DIM = HEAD_DIM // 4
ROPE_THETA = 500000.0
ATTN_SCALE = HEAD_DIM ** -0.5
Q_SCALE = ATTN_SCALE * math.log2(math.e)
POOL_WINDOWS = (2, 4, 8, 16)
POOL_GROUP_DIM = 128
POOL_BUF = max(POOL_WINDOWS) - 1
POOL_HDR = POOL_BUF + 1
N_EXPERT_GROUPS = 4
EXPERTS_PER_GROUP = 4
N_EXPERTS = N_EXPERT_GROUPS * EXPERTS_PER_GROUP
RMS_EPS = 1e-6
LAM_INIT = 0.8 - 0.6 * math.exp(-0.3 * 0)

LANES = 128
ISSUE_UNROLL = 8
ONES_ROWS = 16
FFN_ROW_CHUNKS = 2
VMEM_LIMIT = 56 * 1024 * 1024
NEG_BIG = -1e30

F32 = jnp.float32
BF16 = jnp.bfloat16


def _params(semantics):
    return pltpu.CompilerParams(dimension_semantics=semantics, vmem_limit_bytes=VMEM_LIMIT)


def _const_spec(shape):
    nd = len(shape)
    return pl.BlockSpec(shape, lambda *_: (0,) * nd)


def _rms(x, g):
    ms = jnp.mean(x * x, axis=-1, keepdims=True)
    return (x * lax.rsqrt(ms + RMS_EPS)) * g


def _inproj_kernel(x_ref, g_ref, w_ref, cos_ref, sa_ref, sb_ref, pbuf_ref, pw_ref, ps_ref,
                   q_ref, k_ref, v_ref, py_ref, npool_ref, ext_ref, carry_ref, *, pos0, n_heads):
    t = pl.program_id(0)
    b = pl.program_id(1)
    tm = x_ref.shape[0]
    pool_w = ext_ref.shape[1]
    qk_w = n_heads * 2 * HEAD_DIM

    h = _rms(x_ref[...], g_ref[...]).astype(BF16)

    @pl.when(t == 0)
    def _():
        ext_ref[0:POOL_HDR, :] = pbuf_ref[...]

    @pl.when(t > 0)
    def _():
        ext_ref[0:POOL_HDR, :] = carry_ref[b]

    ext_ref[POOL_HDR:, :] = jnp.dot(h, w_ref[:, 0:pool_w], preferred_element_type=F32)
    tail = ext_ref[tm:tm + POOL_HDR, :]
    carry_ref[b] = tail
    npool_ref[...] = tail

    cos = cos_ref[...]
    sa = sa_ref[...]
    sb = sb_ref[...]
    col_chunk = 4 * LANES
    for which, out_ref in ((0, q_ref), (1, k_ref)):
        for c in range(0, qk_w, col_chunk):
            base = pool_w + which * qk_w + c
            z = jnp.dot(h, w_ref[:, base:base + col_chunk], preferred_element_type=F32)
            for j in range(0, col_chunk, LANES):
                zc = z[:, j:j + LANES]
                rot = (zc * cos + pltpu.roll(zc, LANES - ROT_DIM // 2, 1) * sa
                       + pltpu.roll(zc, ROT_DIM // 2, 1) * sb)
                if which == 0:
                    out_ref[:, c + j:c + j + LANES] = (rot * Q_SCALE).astype(BF16)
                else:
                    out_ref[:, c + j:c + j + LANES] = rot

    v_w = v_ref.shape[1]
    for c in range(0, v_w, col_chunk):
        base = pool_w + 2 * qk_w + c
        v_ref[:, c:c + col_chunk] = jnp.dot(h, w_ref[:, base:base + col_chunk], preferred_element_type=F32)

    pos = pos0 + t * tm + lax.broadcasted_iota(jnp.int32, (tm, 1), 0)
    for g, w in enumerate(POOL_WINDOWS):
        sl = slice(g * POOL_GROUP_DIM, (g + 1) * POOL_GROUP_DIM)
        u = ext_ref[POOL_HDR:POOL_HDR + tm, sl]
        ws = u
        for j in range(1, w):
            ws = ws + ext_ref[POOL_HDR - j:POOL_HDR - j + tm, sl]
        cnt = jnp.minimum(w, pos + 1).astype(F32)
        d = (ws / cnt - u).astype(BF16)
        y = jnp.dot(d, pw_ref[g], preferred_element_type=F32) * ps_ref[:, sl]
        py_ref[:, sl] = y.astype(BF16)


def _inproj(x, pool_hdr, pos0, norm_g, w_a, pool_w, pool_scale, n_heads):
    B, T, D = x.shape
    P = pool_scale.shape[-1]
    qk_w = n_heads * 2 * HEAD_DIM
    v_w = n_heads * V_DIM
    tm = min(T, 512)
    assert T % tm == 0 and T >= POOL_HDR and tm % 16 == 0
    nt = T // tm

    pos = pos0 + jnp.arange(T)
    inv_freq = jnp.power(ROPE_THETA, -jnp.arange(0, ROT_DIM, 2, dtype=F32) / ROT_DIM)
    ang = pos.astype(F32)[:, None] * inv_freq[None, :]
    cos, sin = jnp.cos(ang), jnp.sin(ang)
    half = ROT_DIM // 2
    ones = jnp.ones((T, HEAD_DIM - ROT_DIM), F32)
    zeros_h = jnp.zeros((T, half), F32)
    zeros_r = jnp.zeros((T, HEAD_DIM - ROT_DIM), F32)
    cos_t = jnp.tile(jnp.concatenate([cos, cos, ones], axis=1), (1, LANES // HEAD_DIM))
    sa_t = jnp.tile(jnp.concatenate([-sin, zeros_h, zeros_r], axis=1), (1, LANES // HEAD_DIM))
    sb_t = jnp.tile(jnp.concatenate([zeros_h, sin, zeros_r], axis=1), (1, LANES // HEAD_DIM))

    row = lambda w: pl.BlockSpec((None, tm, w), lambda t, b: (b, t, 0))
    tab = pl.BlockSpec((tm, LANES), lambda t, b: (t, 0))
    hdr = pl.BlockSpec((None, POOL_HDR, P), lambda t, b: (b, 0, 0))
    hdr_out = pl.BlockSpec((None, POOL_HDR, P), lambda t, b: (jnp.where(t == nt - 1, b, 0), 0, 0))
    kern = functools.partial(_inproj_kernel, pos0=pos0, n_heads=n_heads)
    return pl.pallas_call(
        kern,
        grid=(nt, B),
        in_specs=[row(D), _const_spec((1, D)), _const_spec(w_a.shape), tab, tab, tab, hdr,
                  _const_spec(pool_w.shape), _const_spec((1, P))],
        out_specs=[row(qk_w), row(qk_w), row(v_w), row(P), hdr_out],
        out_shape=[jax.ShapeDtypeStruct((B, T, qk_w), BF16),
                   jax.ShapeDtypeStruct((B, T, qk_w), F32),
                   jax.ShapeDtypeStruct((B, T, v_w), F32),
                   jax.ShapeDtypeStruct((B, T, P), BF16),
                   jax.ShapeDtypeStruct((B, POOL_HDR, P), F32)],
        scratch_shapes=[pltpu.VMEM((POOL_HDR + tm, P), F32), pltpu.VMEM((B, POOL_HDR, P), F32)],
        compiler_params=_params(("arbitrary", "arbitrary")),
        name="inproj",
    )(x, norm_g.reshape(1, D), w_a, cos_t, sa_t, sb_t, pool_hdr, pool_w, pool_scale.reshape(1, P))


def _lambda(lam_ref):
    lv = lam_ref[...]
    a = jnp.sum(lv[0:1] * lv[1:2], axis=-1, keepdims=True)
    c = jnp.sum(lv[2:3] * lv[3:4], axis=-1, keepdims=True)
    return jnp.exp(a) - jnp.exp(c) + LAM_INIT


def _split_maps(q):
    lane = lax.broadcasted_iota(jnp.int32, q.shape, 1)
    zero = jnp.zeros_like(q)
    return jnp.where(lane < HEAD_DIM, q, zero), jnp.where(lane >= HEAD_DIM, q, zero)


def _nt_dot(a, b):
    return lax.dot_general(a, b, (((1,), (1,)), ((), ())), preferred_element_type=F32)


def _head_out(o, g):
    ms = jnp.mean(o * o, axis=-1, keepdims=True)
    return ((o * lax.rsqrt(ms + RMS_EPS)) * g) * (1.0 - LAM_INIT)


def _attn_prompt_kernel(q_ref, k_ref, v_ref, lam_ref, g_ref, o_ref,
                        kb_ref, vt_ref, qt_ref, acc1_ref, acc2_ref, stat_ref, sa_ref, sb_ref, sc_ref, *, tq, tk):
    T = q_ref.shape[0]
    nq = T // tq
    half = tq // 2
    assert tk == half

    kb_ref[...] = k_ref[...].astype(BF16)
    ones_row = jnp.where(lax.broadcasted_iota(jnp.int32, (ONES_ROWS, tk), 0) == 0, 1.0, 0.0).astype(BF16)
    for c in range(T // tk):
        vt_ref[c, 0:V_DIM, :] = v_ref[c * tk:(c + 1) * tk, :].T.astype(BF16)
        vt_ref[c, V_DIM:, :] = ones_row
    for c in range(nq):
        qt = q_ref[c * tq:(c + 1) * tq, :].astype(F32).T.astype(BF16)
        row = lax.broadcasted_iota(jnp.int32, qt.shape, 0)
        zero = jnp.zeros_like(qt)
        qt_ref[c, 0] = jnp.where(row < HEAD_DIM, qt, zero)
        qt_ref[c, 1] = jnp.where(row >= HEAD_DIM, qt, zero)

    def scores(j, s_ref, i, right_half=False):
        kj = kb_ref[pl.ds(pl.multiple_of(j * tk, tk), tk), :]
        for c in range(2):
            if right_half:
                s_ref[c, :, half:] = jnp.dot(kj, qt_ref[i, c, :, half:], preferred_element_type=F32)
            else:
                s_ref[c] = jnp.dot(kj, qt_ref[i, c], preferred_element_type=F32)

    def softmax_pv(j, s_ref, kind="full", first=False):
        vtj = vt_ref[j]
        cols = slice(half, tq) if kind == "diag2" else slice(0, tq)
        width = cols.stop - cols.start
        if kind != "full":
            key_chunk = lax.broadcasted_iota(jnp.int32, (tk, width), 0) // CHUNK
            qry_chunk = lax.broadcasted_iota(jnp.int32, (tk, width), 1) // CHUNK
            mask = key_chunk <= qry_chunk
        for c, acc_ref in ((0, acc1_ref), (1, acc2_ref)):
            s = s_ref[c, :, cols]
            if kind != "full":
                s = jnp.where(mask, s, NEG_BIG)
            smax = jnp.max(s, axis=0, keepdims=True)
            if first:
                mn = smax
            else:
                m = stat_ref[c:c + 1, cols]
                mn = jnp.maximum(m, smax)
                alpha = jnp.exp2(m - mn)
            p = jnp.exp2(s - mn)
            stat_ref[c:c + 1, cols] = mn
            pv = jnp.dot(vtj, p.astype(BF16), preferred_element_type=F32)
            if first:
                acc_ref[:, cols] = pv
            else:
                acc_ref[:, cols] = alpha * acc_ref[:, cols] + pv

    lam = _lambda(lam_ref)
    scores(0, sc_ref, 0)

    def q_tile(i, _):
        nxt = jnp.minimum(i + 1, nq - 1)

        @pl.when(i == 0)
        def _():
            scores(1, sa_ref, i, right_half=True)
            softmax_pv(0, sc_ref, "diag1", first=True)
            scores(0, sc_ref, nxt)
            softmax_pv(1, sa_ref, "diag2")

        @pl.when(i > 0)
        def _():
            scores(1, sa_ref, i)
            softmax_pv(0, sc_ref, "full", first=True)

            def pair(p, _):
                t = 2 * p + 1
                scores(t + 1, sb_ref, i)
                softmax_pv(t, sa_ref)
                scores(t + 2, sa_ref, i)
                softmax_pv(t + 1, sb_ref)
                return 0

            lax.fori_loop(0, i - 1, pair, 0)

            d1 = 2 * i
            scores(d1, sb_ref, i)
            softmax_pv(d1 - 1, sa_ref)
            scores(d1 + 1, sa_ref, i, right_half=True)
            softmax_pv(d1, sb_ref, "diag1")
            scores(0, sc_ref, nxt)
            softmax_pv(d1 + 1, sa_ref, "diag2")

        l1 = acc1_ref[V_DIM:V_DIM + 1, :]
        l2 = acc2_ref[V_DIM:V_DIM + 1, :]
        o = acc1_ref[0:V_DIM, :] / l1 - lam * (acc2_ref[0:V_DIM, :] / l2)
        ms = jnp.mean(o * o, axis=0, keepdims=True)
        o = ((o * lax.rsqrt(ms + RMS_EPS)) * g_ref[...]) * (1.0 - LAM_INIT)
        o_ref[pl.ds(pl.multiple_of(i * tq, tq), tq), :] = o.T.astype(BF16)
        return 0

    lax.fori_loop(0, nq, q_tile, 0)


def _attn_prompt(q, k, v, lam_vecs, subln_g, n_heads):
    B, T, _ = q.shape
    tq = min(T, 1024)
    tk = tq // 2
    assert T % tq == 0 and tk % LANES == 0 and tk % CHUNK == 0
    blk = pl.BlockSpec((None, T, V_DIM), lambda b, h: (b, 0, h))
    score_buf = pltpu.VMEM((2, tk, tq), F32)
    acc_buf = pltpu.VMEM((V_DIM + ONES_ROWS, tq), F32)
    return pl.pallas_call(
        functools.partial(_attn_prompt_kernel, tq=tq, tk=tk),
        grid=(B, n_heads),
        in_specs=[blk, blk, blk, _const_spec(lam_vecs.shape), _const_spec((V_DIM, 1))],
        out_specs=blk,
        out_shape=jax.ShapeDtypeStruct((B, T, n_heads * V_DIM), BF16),
        scratch_shapes=[pltpu.VMEM((T, V_DIM), BF16), pltpu.VMEM((T // tk, V_DIM + ONES_ROWS, tk), BF16),
                        pltpu.VMEM((T // tq, 2, V_DIM, tq), BF16), acc_buf, acc_buf,
                        pltpu.VMEM((2, tq), F32), score_buf, score_buf, score_buf],
        compiler_params=_params(("arbitrary", "arbitrary")),
        name="attn_prompt",
    )(q, k, v, lam_vecs, subln_g.reshape(V_DIM, 1))


def _attn_sample_kernel(q_ref, ck_ref, cv_ref, nk_ref, nv_ref, lam_ref, g_ref, o_ref, *, past, n_heads):
    S = q_ref.shape[0]
    P = past
    q_chunk = (past + lax.broadcasted_iota(jnp.int32, (S, 1), 0)) // CHUNK
    mask_c = (lax.broadcasted_iota(jnp.int32, (S, P), 1) // CHUNK) <= q_chunk
    mask_n = ((past + lax.broadcasted_iota(jnp.int32, (S, S), 1)) // CHUNK) <= q_chunk
    lam = _lambda(lam_ref)
    for h in range(n_heads):
        cols = slice(h * V_DIM, (h + 1) * V_DIM)
        q1, q2 = _split_maps(q_ref[:, cols])
        ck = ck_ref[pl.ds(h, P, stride=n_heads), :].astype(BF16)
        cv = cv_ref[pl.ds(h, P, stride=n_heads), :].astype(BF16)
        nk = nk_ref[:, cols].astype(BF16)
        nv = nv_ref[:, cols].astype(BF16)
        outs = []
        for qm in (q1, q2):
            sc = jnp.where(mask_c, _nt_dot(qm, ck), NEG_BIG)
            sn = jnp.where(mask_n, _nt_dot(qm, nk), NEG_BIG)
            m = jnp.maximum(jnp.max(sc, axis=-1, keepdims=True), jnp.max(sn, axis=-1, keepdims=True))
            pc = jnp.exp2(sc - m)
            pn = jnp.exp2(sn - m)
            l = jnp.sum(pc, axis=-1, keepdims=True) + jnp.sum(pn, axis=-1, keepdims=True)
            acc = (jnp.dot(pc.astype(BF16), cv, preferred_element_type=F32)
                   + jnp.dot(pn.astype(BF16), nv, preferred_element_type=F32))
            outs.append(acc / l)
        o = outs[0] - lam * outs[1]
        o_ref[:, cols] = _head_out(o, g_ref[...]).astype(BF16)


def _attn_sample(q, k, v, cache_k, cache_v, lam_vecs, subln_g, n_heads):
    B, S, W = q.shape
    P = cache_k.shape[1] // n_heads
    nblk = pl.BlockSpec((None, S, W), lambda b: (b, 0, 0))
    cblk = pl.BlockSpec((None, P * n_heads, V_DIM), lambda b: (b, 0, 0))
    kern = functools.partial(_attn_sample_kernel, past=P, n_heads=n_heads)
    return pl.pallas_call(
        kern,
        grid=(B,),
        in_specs=[nblk, cblk, cblk, nblk, nblk, _const_spec(lam_vecs.shape), _const_spec((1, V_DIM))],
        out_specs=nblk,
        out_shape=jax.ShapeDtypeStruct((B, S, W), BF16),
        compiler_params=_params(("arbitrary",)),
        name="attn_sample",
    )(q, cache_k, cache_v, k, v, lam_vecs, subln_g.reshape(1, V_DIM))


def _merge_kernel(x_ref, py_ref, at_ref, gmix_ref, wg_ref, bg_ref, wpb_ref, wab_ref, wo_ref,
                  gffn_ref, wr_ref, br_ref, tri_ref,
                  x1_ref, ri_ref, wc_ref, cnt_ref, wrow_ref, run_ref):
    step = pl.program_id(0)
    tm, D = x_ref.shape

    @pl.when(step == 0)
    def _():
        wrow_ref[...] = jnp.zeros_like(wrow_ref)
        run_ref[...] = jnp.zeros_like(run_ref)

    x = x_ref[...]
    h = _rms(x, gmix_ref[...]).astype(BF16)
    gl = jnp.dot(h, wg_ref[...], preferred_element_type=F32) + bg_ref[...]
    a = jnp.dot(py_ref[...], wpb_ref[...], preferred_element_type=F32)
    bb = jnp.dot(at_ref[...], wab_ref[...], preferred_element_type=F32)
    merged = jax.nn.sigmoid(gl[:, :D]) * a + jax.nn.sigmoid(gl[:, D:]) * bb
    x1 = x + jnp.dot(merged.astype(BF16), wo_ref[...], preferred_element_type=F32)
    x1_ref[...] = x1

    h2 = _rms(x1, gffn_ref[...]).astype(BF16)
    logits = jnp.dot(h2, wr_ref[...], preferred_element_type=F32) + br_ref[...]
    lt = logits.T
    G = N_EXPERT_GROUPS
    E = EXPERTS_PER_GROUP
    best = lt[0:1]
    gsel = jnp.zeros((1, tm), jnp.int32)
    for g in range(1, G):
        better = lt[g:g + 1] > best
        gsel = jnp.where(better, g, gsel)
        best = jnp.where(better, lt[g:g + 1], best)
    gden = jnp.zeros((1, tm), F32)
    for g in range(G):
        gden = gden + jnp.exp(lt[g:g + 1] - best)
    g_w = 1.0 / gden
    el = [lt[G + e:G + e + 1] for e in range(E)]
    for g in range(1, G):
        sel = gsel == g
        el = [jnp.where(sel, lt[G + g * E + e:G + g * E + e + 1], el[e]) for e in range(E)]
    v0 = el[0]
    i0 = jnp.zeros((1, tm), jnp.int32)
    for e in range(1, E):
        better = el[e] > v0
        i0 = jnp.where(better, e, i0)
        v0 = jnp.where(better, el[e], v0)
    v1 = jnp.full((1, tm), -jnp.inf, F32)
    i1 = jnp.zeros((1, tm), jnp.int32)
    for e in range(E):
        better = (i0 != e) & (el[e] > v1)
        i1 = jnp.where(better, e, i1)
        v1 = jnp.where(better, el[e], v1)
    ex = jnp.exp(v1 - v0)
    w0 = (1.0 / (1.0 + ex)) * g_w
    w1 = (ex / (1.0 + ex)) * g_w
    e0 = gsel * E + i0
    e1 = gsel * E + i1

    eid = lax.broadcasted_iota(jnp.int32, (N_EXPERTS, tm), 0)
    oh0 = eid == e0
    oh1 = eid == e1
    oh = jnp.where(oh0 | oh1, 1.0, 0.0).astype(BF16)
    pref = jnp.dot(oh, tri_ref[...], preferred_element_type=F32)
    rank = run_ref[...] + pref - 1.0
    r0 = jnp.sum(jnp.where(oh0, rank, 0.0), axis=0, keepdims=True)
    r1 = jnp.sum(jnp.where(oh1, rank, 0.0), axis=0, keepdims=True)
    run_ref[...] = run_ref[...] + pref[:, tm - 1:tm]
    cnt_ref[...] = jnp.broadcast_to(run_ref[...], cnt_ref.shape).astype(jnp.int32)

    ri_ref[0:1, :] = e0
    ri_ref[1:2, :] = e1
    ri_ref[2:3, :] = r0.astype(jnp.int32)
    ri_ref[3:4, :] = r1.astype(jnp.int32)
    wrow_ref[0:1, :] = w0
    wrow_ref[1:2, :] = w1
    wc_ref[...] = wrow_ref[...].T


def _merge(x, py, at, p):
    N, D = x.shape
    tm = min(N, 512)
    assert N % tm == 0
    tri = (jnp.arange(tm)[:, None] <= jnp.arange(tm)[None, :]).astype(BF16)
    row = lambda w: pl.BlockSpec((tm, w), lambda i: (i, 0))
    consts = [p["gmix"], p["w_gate"], p["b_gate"], p["w_pb"], p["w_ab"], p["w_out"],
              p["gffn"], p["w_router"], p["b_router"], tri]
    return pl.pallas_call(
        _merge_kernel,
        grid=(N // tm,),
        in_specs=[row(D), row(py.shape[1]), row(at.shape[1])] + [_const_spec(c.shape) for c in consts],
        out_specs=[row(D), pl.BlockSpec((4, tm), lambda i: (0, i)), row(LANES),
                   _const_spec((N_EXPERTS, LANES))],
        out_shape=[jax.ShapeDtypeStruct((N, D), F32),
                   jax.ShapeDtypeStruct((4, N), jnp.int32),
                   jax.ShapeDtypeStruct((N, LANES), F32),
                   jax.ShapeDtypeStruct((N_EXPERTS, LANES), jnp.int32)],
        scratch_shapes=[pltpu.VMEM((LANES, tm), F32), pltpu.VMEM((N_EXPERTS, 1), F32)],
        compiler_params=_params(("arbitrary",)),
        name="merge",
    )(x, py, at, *consts)


def _row_tiles_store(ref_view, x, row0=0):
    rows = x.shape[0]
    n = x.shape[1] // LANES
    for s in range(n):
        ref_view[pl.ds(row0 * n + s, rows, stride=n), :] = x[:, s * LANES:(s + 1) * LANES]


def _row_tiles_load(ref_view, rows, width, row0=0):
    n = width // LANES
    return jnp.concatenate([ref_view[pl.ds(row0 * n + s, rows, stride=n), :] for s in range(n)], axis=1)


def _dispatch_kernel(pos_ref, x1_ref, g_ref, xs_ref, buf_ref, sem_ref):
    i = pl.program_id(0)
    n = pl.num_programs(0)
    td, D = x1_ref.shape
    rt = D // LANES
    slot = i % 2

    def drain(s):
        for _ in range(2):
            pltpu.make_async_copy(buf_ref.at[s], xs_ref.at[pl.ds(0, td * rt)], sem_ref.at[s]).wait()

    @pl.when(i >= 2)
    def _():
        drain(slot)

    _row_tiles_store(buf_ref.at[slot], _rms(x1_ref[...], g_ref[...]))

    def issue(r8, _):
        for u in range(ISSUE_UNROLL):
            r = r8 * ISSUE_UNROLL + u
            for c in range(2):
                dst = pl.multiple_of(pos_ref[0, c * td + r] * rt, rt)
                pltpu.make_async_copy(buf_ref.at[slot, pl.ds(r * rt, rt)], xs_ref.at[pl.ds(dst, rt)],
                                      sem_ref.at[slot]).start(priority=c)
        return 0

    lax.fori_loop(0, td // ISSUE_UNROLL, issue, 0)

    @pl.when(i == n - 1)
    def _():
        drain(slot)

        @pl.when(n >= 2)
        def _():
            drain(1 - slot)


def _dispatch(x1, pos, gffn):
    N, D = x1.shape
    td = min(N, 512)
    assert N % td == 0 and td % ISSUE_UNROLL == 0
    nt = N // td
    rt = D // LANES
    pos3 = pos.reshape(2, nt, td).transpose(1, 0, 2).reshape(nt, 1, 2 * td)
    return pl.pallas_call(
        _dispatch_kernel,
        grid=(nt,),
        in_specs=[pl.BlockSpec((None, 1, 2 * td), lambda i: (i, 0, 0), memory_space=pltpu.SMEM),
                  pl.BlockSpec((td, D), lambda i: (i, 0)), _const_spec((1, D))],
        out_specs=pl.BlockSpec(memory_space=pl.ANY),
        out_shape=jax.ShapeDtypeStruct((2 * N * rt, LANES), F32),
        scratch_shapes=[pltpu.VMEM((2, td * rt, LANES), F32), pltpu.SemaphoreType.DMA((2,))],
        compiler_params=_params(("arbitrary",)),
        name="dispatch",
    )(pos3, x1, gffn)


def _ffn_kernel(tile_ref, ex_ref, lo_ref, hi_ref, first_ref, xs_ref, wg_ref, wu_ref, wd_ref, ys_ref):
    i = pl.program_id(0)
    D = wg_ref.shape[0]
    rt = D // LANES
    tf = xs_ref.shape[0] // rt
    lo = lo_ref[i]
    hi = hi_ref[i]

    @pl.when(first_ref[i] == 1)
    def _():
        ys_ref[...] = jnp.zeros_like(ys_ref)

    @pl.when(hi > lo)
    def _():
        rc = tf // FFN_ROW_CHUNKS
        for ch in range(FFN_ROW_CHUNKS):
            r0 = ch * rc
            xb = _row_tiles_load(xs_ref, rc, D, r0).astype(BF16)
            gate = jnp.dot(xb, wg_ref[...], preferred_element_type=F32)
            up = jnp.dot(xb, wu_ref[...], preferred_element_type=F32)
            hid = (jax.nn.silu(gate) * up).astype(BF16)
            y = jnp.dot(hid, wd_ref[...], preferred_element_type=F32)
            rows = tile_ref[i] * tf + r0 + lax.broadcasted_iota(jnp.int32, (rc, 1), 0)
            mine = (rows >= lo) & (rows < hi)
            _row_tiles_store(ys_ref, jnp.where(mine, y, _row_tiles_load(ys_ref, rc, D, r0)), r0)


def _ffn(xs, sched, w_gate, w_up, w_down, tf):
    D, Hd = w_gate.shape[1:]
    rt = D // LANES
    n_items = sched[0].shape[0]
    rows = pl.BlockSpec((tf * rt, LANES), lambda i, tile, ex, lo, hi, first: (tile[i], 0))
    wspec = lambda a, b_: pl.BlockSpec((None, a, b_), lambda i, tile, ex, lo, hi, first: (ex[i], 0, 0))
    return pl.pallas_call(
        _ffn_kernel,
        grid_spec=pltpu.PrefetchScalarGridSpec(
            num_scalar_prefetch=5,
            grid=(n_items,),
            in_specs=[rows, wspec(D, Hd), wspec(D, Hd), wspec(Hd, D)],
            out_specs=rows,
        ),
        out_shape=jax.ShapeDtypeStruct(xs.shape, F32),
        compiler_params=_params(("arbitrary",)),
        name="expert_ffn",
    )(*sched, xs, w_gate, w_up, w_down)


def _schedule(counts, M, tf):
    n_tiles = M // tf
    n_items = n_tiles + N_EXPERTS - 1
    ends = jnp.cumsum(counts)
    starts = ends - counts
    first_tile = starts // tf
    last_tile = jnp.maximum(ends - 1, 0) // tf
    per = jnp.where(counts > 0, last_tile - first_tile + 1, 0)
    item_end = jnp.cumsum(per)
    item_start = item_end - per
    total = item_end[-1]
    idx = jnp.arange(n_items, dtype=jnp.int32)
    valid = idx < total
    idx_c = jnp.minimum(idx, total - 1)
    ex = jnp.minimum(jnp.sum(item_end[None, :] <= idx_c[:, None], axis=1), N_EXPERTS - 1).astype(jnp.int32)
    onehot = ex[:, None] == jnp.arange(N_EXPERTS)[None, :]
    pick = lambda table: jnp.sum(jnp.where(onehot, table[None, :], 0), axis=1)
    tile = (pick(first_tile) + idx_c - pick(item_start)).astype(jnp.int32)
    lo = jnp.where(valid, pick(starts), 0).astype(jnp.int32)
    hi = jnp.where(valid, pick(ends), 0).astype(jnp.int32)
    prev = jnp.concatenate([jnp.full((1,), -1, jnp.int32), tile[:-1]])
    first = (valid & (tile != prev)).astype(jnp.int32)
    return (tile, ex, lo, hi, first), starts


def _combine_kernel(pos_ref, nxt_ref, x1_ref, wc_ref, g_ref, ys_ref, o_ref, buf_ref, sem_ref):
    i = pl.program_id(0)
    n = pl.num_programs(0)
    tc, D = x1_ref.shape
    rt = D // LANES
    slot = i % 2

    def gather(idx_ref, s):
        def issue(r8, _):
            for u in range(ISSUE_UNROLL):
                r = r8 * ISSUE_UNROLL + u
                for c in range(2):
                    src = pl.multiple_of(idx_ref[0, c * tc + r] * rt, rt)
                    pltpu.make_async_copy(ys_ref.at[pl.ds(src, rt)], buf_ref.at[s, c, pl.ds(r * rt, rt)],
                                          sem_ref.at[s]).start(priority=c)
            return 0
        lax.fori_loop(0, tc // ISSUE_UNROLL, issue, 0)

    @pl.when(i == 0)
    def _():
        gather(pos_ref, slot)

    @pl.when(i + 1 < n)
    def _():
        gather(nxt_ref, 1 - slot)

    for c in range(2):
        pltpu.make_async_copy(ys_ref.at[pl.ds(0, tc * rt)], buf_ref.at[slot, c], sem_ref.at[slot]).wait()

    wc = wc_ref[...]
    r0 = _row_tiles_load(buf_ref.at[slot, 0], tc, D)
    r1 = _row_tiles_load(buf_ref.at[slot, 1], tc, D)
    y = x1_ref[...] + (wc[:, 0:1] * r0 + wc[:, 1:2] * r1)
    o_ref[...] = _rms(y, g_ref[...])


def _combine(x1, pos, wcol, ys, final_g):
    N, D = x1.shape
    tc = min(N, 256)
    assert N % tc == 0 and tc % ISSUE_UNROLL == 0
    nt = N // tc
    rt = D // LANES
    pos3 = pos.reshape(2, nt, tc).transpose(1, 0, 2).reshape(nt, 1, 2 * tc)
    smem = lambda f: pl.BlockSpec((None, 1, 2 * tc), f, memory_space=pltpu.SMEM)
    return pl.pallas_call(
        _combine_kernel,
        grid=(nt,),
        in_specs=[smem(lambda i: (i, 0, 0)), smem(lambda i: (jnp.minimum(i + 1, nt - 1), 0, 0)),
                  pl.BlockSpec((tc, D), lambda i: (i, 0)), pl.BlockSpec((tc, LANES), lambda i: (i, 0)),
                  _const_spec((1, D)), pl.BlockSpec(memory_space=pl.ANY)],
        out_specs=pl.BlockSpec((tc, D), lambda i: (i, 0)),
        out_shape=jax.ShapeDtypeStruct((N, D), F32),
        scratch_shapes=[pltpu.VMEM((2, 2, tc * rt, LANES), F32), pltpu.SemaphoreType.DMA((2,))],
        compiler_params=_params(("arbitrary",)),
        name="combine",
    )(pos3, pos3, x1, wcol, final_g, ys)


def _moe_and_norm(x1, ri, wcol, counts, p):
    N, D = x1.shape
    tf = min(2 * N, 512)
    sched, starts = _schedule(counts, 2 * N, tf)
    hit = ri[0:2][:, None, :] == jnp.arange(N_EXPERTS, dtype=jnp.int32)[None, :, None]
    pos = jnp.sum(jnp.where(hit, starts.astype(jnp.int32)[None, :, None], 0), axis=1) + ri[2:4]
    xs = _dispatch(x1, pos, p["gffn"])
    ys = _ffn(xs, sched, p["w_eg"], p["w_eu"], p["w_ed"], tf)
    return _combine(x1, pos, wcol, ys, p["final_g"])


def _stream(x, pos0, pool_hdr, cache, p, n_heads):
    B, T, D = x.shape
    q, k, v, py, npool = _inproj(x, pool_hdr, pos0, p["gmix"], p["w_a"], p["pool_w"], p["pool_scale"], n_heads)
    if cache is None:
        at = _attn_prompt(q, k, v, p["lam_vecs"], p["subln_g"], n_heads)
    else:
        at = _attn_sample(q, k, v, cache[0], cache[1], p["lam_vecs"], p["subln_g"], n_heads)
    N = B * T
    x1, ri, wcol, cnt = _merge(x.reshape(N, D), py.reshape(N, -1), at.reshape(N, -1), p)
    y = _moe_and_norm(x1, ri, wcol, cnt[:, 0], p)
    return (y.reshape(B, T, D), k.reshape(1, B, T, n_heads, 2 * HEAD_DIM), v.reshape(1, B, T, n_heads, V_DIM),
            npool[None, :, 1:, :])


def kernel(x_prompt, x_sample, cache_k, cache_v, state_pool, norm_mix_g, w_in, b_gate, lambda_q1, lambda_k1,
           lambda_q2, lambda_k2, subln_g, pool_w, pool_scale, w_pool_branch, w_attn_branch, w_out, norm_ffn_g,
           w_router_group, b_router_group, w_router_expert, b_router_expert, w_expert_gate, w_expert_up,
           w_expert_down, final_norm_g):
    assert w_in.shape[0] == 1, "single layer"
    D = x_prompt.shape[-1]
    n_heads = cache_k.shape[3]
    P = pool_scale.shape[-1]
    past = cache_k.shape[2]
    n_a = P + 2 * n_heads * 2 * HEAD_DIM + n_heads * V_DIM

    wr = jnp.concatenate([w_router_group[0], jnp.transpose(w_router_expert[0], (1, 0, 2)).reshape(D, N_EXPERTS)],
                         axis=1)
    br = jnp.concatenate([b_router_group[0], b_router_expert[0].reshape(N_EXPERTS)])
    n_r = wr.shape[1]
    p = dict(
        gmix=norm_mix_g[0].reshape(1, D),
        w_a=w_in[0, :, :n_a].astype(BF16),
        w_gate=w_in[0, :, n_a:].astype(BF16),
        b_gate=b_gate[0].reshape(1, 2 * D),
        lam_vecs=jnp.stack([lambda_q1[0], lambda_k1[0], lambda_q2[0], lambda_k2[0]]),
        subln_g=subln_g[0],
        pool_w=pool_w[0].astype(BF16),
        pool_scale=pool_scale[0],
        w_pb=w_pool_branch[0].astype(BF16),
        w_ab=w_attn_branch[0].astype(BF16),
        w_out=w_out[0].astype(BF16),
        gffn=norm_ffn_g[0].reshape(1, D),
        w_router=jnp.pad(wr, ((0, 0), (0, LANES - n_r))).astype(BF16),
        b_router=jnp.pad(br, (0, LANES - n_r)).reshape(1, LANES),
        w_eg=w_expert_gate[0].astype(BF16),
        w_eu=w_expert_up[0].astype(BF16),
        w_ed=w_expert_down[0].astype(BF16),
        final_g=final_norm_g.reshape(1, D),
    )

    Bp = x_prompt.shape[0]
    Bs = x_sample.shape[0]
    hdr_p = jnp.zeros((Bp, POOL_HDR, P), F32)
    hdr_s = jnp.pad(state_pool[0], ((0, 0), (1, 0), (0, 0)))
    cache = (cache_k[0].reshape(Bs, past * n_heads, -1), cache_v[0].reshape(Bs, past * n_heads, -1))

    yp, kp, vp, pp = _stream(x_prompt, 0, hdr_p, None, p, n_heads)
    ys, ks, vs, ps = _stream(x_sample, past, hdr_s, cache, p, n_heads)
    return (yp, ys, kp, vp, pp, ks, vs, ps)
```

```python
import functools
import math

import jax
import jax.numpy as jnp
from jax import lax
from jax.experimental import pallas as pl
from jax.experimental.pallas import tpu as pltpu

CHUNK = 64
HEAD_DIM = 64
V_DIM = 2 * HEAD_DIM
ROT_DIM = HEAD_DIM // 4
ROPE_THETA = 500000.0
ATTN_SCALE = HEAD_DIM ** -0.5
Q_SCALE = ATTN_SCALE * math.log2(math.e)
POOL_WINDOWS = (2, 4, 8, 16)
POOL_GROUP_DIM = 128
POOL_BUF = max(POOL_WINDOWS) - 1
POOL_HDR = POOL_BUF + 1
N_EXPERT_GROUPS = 4
EXPERTS_PER_GROUP = 4
N_EXPERTS = N_EXPERT_GROUPS * EXPERTS_PER_GROUP
RMS_EPS = 1e-6
LAM_INIT = 0.8 - 0.6 * math.exp(-0.3 * 0)

LANES = 128
ISSUE_UNROLL = 8
ONES_ROWS = 16
FFN_ROW_CHUNKS = 2
VMEM_LIMIT = 56 * 1024 * 1024
NEG_BIG = -1e30

F32 = jnp.float32
BF16 = jnp.bfloat16


def _params(semantics):
    return pltpu.CompilerParams(dimension_semantics=semantics, vmem_limit_bytes=VMEM_LIMIT)


def _const_spec(shape):
    nd = len(shape)
    return pl.BlockSpec(shape, lambda *_: (0,) * nd)


def _rms(x, g):
    ms = jnp.mean(x * x, axis=-1, keepdims=True)
    return (x * lax.rsqrt(ms + RMS_EPS)) * g


def _inproj_kernel(x_ref, g_ref, w_ref, cos_ref, sa_ref, sb_ref, pbuf_ref, pw_ref, ps_ref,
                   q_ref, k_ref, v_ref, py_ref, npool_ref, ext_ref, carry_ref, *, pos0, n_heads):
    t = pl.program_id(0)
    b = pl.program_id(1)
    tm = x_ref.shape[0]
    pool_w = ext_ref.shape[1]
    qk_w = n_heads * 2 * HEAD_DIM

    h = _rms(x_ref[...], g_ref[...]).astype(BF16)

    @pl.when(t == 0)
    def _():
        ext_ref[0:POOL_HDR, :] = pbuf_ref[...]

    @pl.when(t > 0)
    def _():
        ext_ref[0:POOL_HDR, :] = carry_ref[b]

    ext_ref[POOL_HDR:, :] = jnp.dot(h, w_ref[:, 0:pool_w], preferred_element_type=F32)
    tail = ext_ref[tm:tm + POOL_HDR, :]
    carry_ref[b] = tail
    npool_ref[...] = tail

    cos = cos_ref[...]
    sa = sa_ref[...]
    sb = sb_ref[...]
    col_chunk = 4 * LANES
    for which, out_ref in ((0, q_ref), (1, k_ref)):
        for c in range(0, qk_w, col_chunk):
            base = pool_w + which * qk_w + c
            z = jnp.dot(h, w_ref[:, base:base + col_chunk], preferred_element_type=F32)
            for j in range(0, col_chunk, LANES):
                zc = z[:, j:j + LANES]
                rot = (zc * cos + pltpu.roll(zc, LANES - ROT_DIM // 2, 1) * sa
                       + pltpu.roll(zc, ROT_DIM // 2, 1) * sb)
                if which == 0:
                    out_ref[:, c + j:c + j + LANES] = (rot * Q_SCALE).astype(BF16)
                else:
                    out_ref[:, c + j:c + j + LANES] = rot

    v_w = v_ref.shape[1]
    for c in range(0, v_w, col_chunk):
        base = pool_w + 2 * qk_w + c
        v_ref[:, c:c + col_chunk] = jnp.dot(h, w_ref[:, base:base + col_chunk], preferred_element_type=F32)

    pos = pos0 + t * tm + lax.broadcasted_iota(jnp.int32, (tm, 1), 0)
    for g, w in enumerate(POOL_WINDOWS):
        sl = slice(g * POOL_GROUP_DIM, (g + 1) * POOL_GROUP_DIM)
        u = ext_ref[POOL_HDR:POOL_HDR + tm, sl]
        ws = u
        for j in range(1, w):
            ws = ws + ext_ref[POOL_HDR - j:POOL_HDR - j + tm, sl]
        cnt = jnp.minimum(w, pos + 1).astype(F32)
        d = (ws / cnt - u).astype(BF16)
        y = jnp.dot(d, pw_ref[g], preferred_element_type=F32) * ps_ref[:, sl]
        py_ref[:, sl] = y.astype(BF16)


def _inproj(x, pool_hdr, pos0, norm_g, w_a, pool_w, pool_scale, n_heads):
    B, T, D = x.shape
    P = pool_scale.shape[-1]
    qk_w = n_heads * 2 * HEAD_DIM
    v_w = n_heads * V_DIM
    tm = min(T, 512)
    assert T % tm == 0 and T >= POOL_HDR and tm % 16 == 0
    nt = T // tm

    pos = pos0 + jnp.arange(T)
    inv_freq = jnp.power(ROPE_THETA, -jnp.arange(0, ROT_DIM, 2, dtype=F32) / ROT_DIM)
    ang = pos.astype(F32)[:, None] * inv_freq[None, :]
    cos, sin = jnp.cos(ang), jnp.sin(ang)
    half = ROT_DIM // 2
    ones = jnp.ones((T, HEAD_DIM - ROT_DIM), F32)
    zeros_h = jnp.zeros((T, half), F32)
    zeros_r = jnp.zeros((T, HEAD_DIM - ROT_DIM), F32)
    cos_t = jnp.tile(jnp.concatenate([cos, cos, ones], axis=1), (1, LANES // HEAD_DIM))
    sa_t = jnp.tile(jnp.concatenate([-sin, zeros_h, zeros_r], axis=1), (1, LANES // HEAD_DIM))
    sb_t = jnp.tile(jnp.concatenate([zeros_h, sin, zeros_r], axis=1), (1, LANES // HEAD_DIM))

    row = lambda w: pl.BlockSpec((None, tm, w), lambda t, b: (b, t, 0))
    tab = pl.BlockSpec((tm, LANES), lambda t, b: (t, 0))
    hdr = pl.BlockSpec((None, POOL_HDR, P), lambda t, b: (b, 0, 0))
    hdr_out = pl.BlockSpec((None, POOL_HDR, P), lambda t, b: (jnp.where(t == nt - 1, b, 0), 0, 0))
    kern = functools.partial(_inproj_kernel, pos0=pos0, n_heads=n_heads)
    return pl.pallas_call(
        kern,
        grid=(nt, B),
        in_specs=[row(D), _const_spec((1, D)), _const_spec(w_a.shape), tab, tab, tab, hdr,
                  _const_spec(pool_w.shape), _const_spec((1, P))],
        out_specs=[row(qk_w), row(qk_w), row(v_w), row(P), hdr_out],
        out_shape=[jax.ShapeDtypeStruct((B, T, qk_w), BF16),
                   jax.ShapeDtypeStruct((B, T, qk_w), F32),
                   jax.ShapeDtypeStruct((B, T, v_w), F32),
                   jax.ShapeDtypeStruct((B, T, P), BF16),
                   jax.ShapeDtypeStruct((B, POOL_HDR, P), F32)],
        scratch_shapes=[pltpu.VMEM((POOL_HDR + tm, P), F32), pltpu.VMEM((B, POOL_HDR, P), F32)],
        compiler_params=_params(("arbitrary", "arbitrary")),
        name="inproj",
    )(x, norm_g.reshape(1, D), w_a, cos_t, sa_t, sb_t, pool_hdr, pool_w, pool_scale.reshape(1, P))


def _lambda(lam_ref):
    lv = lam_ref[...]
    a = jnp.sum(lv[0:1] * lv[1:2], axis=-1, keepdims=True)
    c = jnp.sum(lv[2:3] * lv[3:4], axis=-1, keepdims=True)
    return jnp.exp(a) - jnp.exp(c) + LAM_INIT


def _split_maps(q):
    lane = lax.broadcasted_iota(jnp.int32, q.shape, 1)
    zero = jnp.zeros_like(q)
    return jnp.where(lane < HEAD_DIM, q, zero), jnp.where(lane >= HEAD_DIM, q, zero)


def _nt_dot(a, b):
    return lax.dot_general(a, b, (((1,), (1,)), ((), ())), preferred_element_type=F32)


def _head_out(o, g):
    ms = jnp.mean(o * o, axis=-1, keepdims=True)
    return ((o * lax.rsqrt(ms + RMS_EPS)) * g) * (1.0 - LAM_INIT)


def _attn_prompt_kernel(q_ref, k_ref, v_ref, lam_ref, g_ref, o_ref,
                        kb_ref, vt_ref, qt_ref, acc1_ref, acc2_ref, stat_ref, sa_ref, sb_ref, sc_ref, *, tq, tk):
    T = q_ref.shape[0]
    nq = T // tq
    half = tq // 2
    assert tk == half

    kb_ref[...] = k_ref[...].astype(BF16)
    ones_row = jnp.where(lax.broadcasted_iota(jnp.int32, (ONES_ROWS, tk), 0) == 0, 1.0, 0.0).astype(BF16)
    for c in range(T // tk):
        vt_ref[c, 0:V_DIM, :] = v_ref[c * tk:(c + 1) * tk, :].T.astype(BF16)
        vt_ref[c, V_DIM:, :] = ones_row
    for c in range(nq):
        qt = q_ref[c * tq:(c + 1) * tq, :].astype(F32).T.astype(BF16)
        row = lax.broadcasted_iota(jnp.int32, qt.shape, 0)
        zero = jnp.zeros_like(qt)
        qt_ref[c, 0] = jnp.where(row < HEAD_DIM, qt, zero)
        qt_ref[c, 1] = jnp.where(row >= HEAD_DIM, qt, zero)

    def scores(j, s_ref, i, right_half=False):
        kj = kb_ref[pl.ds(pl.multiple_of(j * tk, tk), tk), :]
        for c in range(2):
            if right_half:
                s_ref[c, :, half:] = jnp.dot(kj, qt_ref[i, c, :, half:], preferred_element_type=F32)
            else:
                s_ref[c] = jnp.dot(kj, qt_ref[i, c], preferred_element_type=F32)

    def softmax_pv(j, s_ref, kind="full", first=False):
        vtj = vt_ref[j]
        cols = slice(half, tq) if kind == "diag2" else slice(0, tq)
        width = cols.stop - cols.start
        if kind != "full":
            key_chunk = lax.broadcasted_iota(jnp.int32, (tk, width), 0) // CHUNK
            qry_chunk = lax.broadcasted_iota(jnp.int32, (tk, width), 1) // CHUNK
            mask = key_chunk <= qry_chunk
        for c, acc_ref in ((0, acc1_ref), (1, acc2_ref)):
            s = s_ref[c, :, cols]
            if kind != "full":
                s = jnp.where(mask, s, NEG_BIG)
            smax = jnp.max(s, axis=0, keepdims=True)
            if first:
                mn = smax
            else:
                m = stat_ref[c:c + 1, cols]
                mn = jnp.maximum(m, smax)
                alpha = jnp.exp2(m - mn)
            p = jnp.exp2(s - mn)
            stat_ref[c:c + 1, cols] = mn
            pv = jnp.dot(vtj, p.astype(BF16), preferred_element_type=F32)
            if first:
                acc_ref[:, cols] = pv
            else:
                acc_ref[:, cols] = alpha * acc_ref[:, cols] + pv

    lam = _lambda(lam_ref)
    scores(0, sc_ref, 0)

    def q_tile(i, _):
        nxt = jnp.minimum(i + 1, nq - 1)

        @pl.when(i == 0)
        def _():
            scores(1, sa_ref, i, right_half=True)
            softmax_pv(0, sc_ref, "diag1", first=True)
            scores(0, sc_ref, nxt)
            softmax_pv(1, sa_ref, "diag2")

        @pl.when(i > 0)
        def _():
            scores(1, sa_ref, i)
            softmax_pv(0, sc_ref, "full", first=True)

            def pair(p, _):
                t = 2 * p + 1
                scores(t + 1, sb_ref, i)
                softmax_pv(t, sa_ref)
                scores(t + 2, sa_ref, i)
                softmax_pv(t + 1, sb_ref)
                return 0

            lax.fori_loop(0, i - 1, pair, 0)

            d1 = 2 * i
            scores(d1, sb_ref, i)
            softmax_pv(d1 - 1, sa_ref)
            scores(d1 + 1, sa_ref, i, right_half=True)
            softmax_pv(d1, sb_ref, "diag1")
            scores(0, sc_ref, nxt)
            softmax_pv(d1 + 1, sa_ref, "diag2")

        l1 = acc1_ref[V_DIM:V_DIM + 1, :]
        l2 = acc2_ref[V_DIM:V_DIM + 1, :]
        o = acc1_ref[0:V_DIM, :] / l1 - lam * (acc2_ref[0:V_DIM, :] / l2)
        ms = jnp.mean(o * o, axis=0, keepdims=True)
        o = ((o * lax.rsqrt(ms + RMS_EPS)) * g_ref[...]) * (1.0 - LAM_INIT)
        o_ref[pl.ds(pl.multiple_of(i * tq, tq), tq), :] = o.T.astype(BF16)
        return 0

    lax.fori_loop(0, nq, q_tile, 0)


def _attn_prompt(q, k, v, lam_vecs, subln_g, n_heads):
    B, T, _ = q.shape
    tq = min(T, 1024)
    tk = tq // 2
    assert T % tq == 0 and tk % LANES == 0 and tk % CHUNK == 0
    blk = pl.BlockSpec((None, T, V_DIM), lambda b, h: (b, 0, h))
    score_buf = pltpu.VMEM((2, tk, tq), F32)
    acc_buf = pltpu.VMEM((V_DIM + ONES_ROWS, tq), F32)
    return pl.pallas_call(
        functools.partial(_attn_prompt_kernel, tq=tq, tk=tk),
        grid=(B, n_heads),
        in_specs=[blk, blk, blk, _const_spec(lam_vecs.shape), _const_spec((V_DIM, 1))],
        out_specs=blk,
        out_shape=jax.ShapeDtypeStruct((B, T, n_heads * V_DIM), BF16),
        scratch_shapes=[pltpu.VMEM((T, V_DIM), BF16), pltpu.VMEM((T // tk, V_DIM + ONES_ROWS, tk), BF16),
                        pltpu.VMEM((T // tq, 2, V_DIM, tq), BF16), acc_buf, acc_buf,
                        pltpu.VMEM((2, tq), F32), score_buf, score_buf, score_buf],
        compiler_params=_params(("arbitrary", "arbitrary")),
        name="attn_prompt",
    )(q, k, v, lam_vecs, subln_g.reshape(V_DIM, 1))


def _attn_sample_kernel(q_ref, ck_ref, cv_ref, nk_ref, nv_ref, lam_ref, g_ref, o_ref, *, past, n_heads):
    S = q_ref.shape[0]
    P = past
    q_chunk = (past + lax.broadcasted_iota(jnp.int32, (S, 1), 0)) // CHUNK
    mask_c = (lax.broadcasted_iota(jnp.int32, (S, P), 1) // CHUNK) <= q_chunk
    mask_n = ((past + lax.broadcasted_iota(jnp.int32, (S, S), 1)) // CHUNK) <= q_chunk
    lam = _lambda(lam_ref)
    for h in range(n_heads):
        cols = slice(h * V_DIM, (h + 1) * V_DIM)
        q1, q2 = _split_maps(q_ref[:, cols])
        ck = ck_ref[pl.ds(h, P, stride=n_heads), :].astype(BF16)
        cv = cv_ref[pl.ds(h, P, stride=n_heads), :].astype(BF16)
        nk = nk_ref[:, cols].astype(BF16)
        nv = nv_ref[:, cols].astype(BF16)
        outs = []
        for qm in (q1, q2):
            sc = jnp.where(mask_c, _nt_dot(qm, ck), NEG_BIG)
            sn = jnp.where(mask_n, _nt_dot(qm, nk), NEG_BIG)
            m = jnp.maximum(jnp.max(sc, axis=-1, keepdims=True), jnp.max(sn, axis=-1, keepdims=True))
            pc = jnp.exp2(sc - m)
            pn = jnp.exp2(sn - m)
            l = jnp.sum(pc, axis=-1, keepdims=True) + jnp.sum(pn, axis=-1, keepdims=True)
            acc = (jnp.dot(pc.astype(BF16), cv, preferred_element_type=F32)
                   + jnp.dot(pn.astype(BF16), nv, preferred_element_type=F32))
            outs.append(acc / l)
        o = outs[0] - lam * outs[1]
        o_ref[:, cols] = _head_out(o, g_ref[...]).astype(BF16)


def _attn_sample(q, k, v, cache_k, cache_v, lam_vecs, subln_g, n_heads):
    B, S, W = q.shape
    P = cache_k.shape[1] // n_heads
    nblk = pl.BlockSpec((None, S, W), lambda b: (b, 0, 0))
    cblk = pl.BlockSpec((None, P * n_heads, V_DIM), lambda b: (b, 0, 0))
    kern = functools.partial(_attn_sample_kernel, past=P, n_heads=n_heads)
    return pl.pallas_call(
        kern,
        grid=(B,),
        in_specs=[nblk, cblk, cblk, nblk, nblk, _const_spec(lam_vecs.shape), _const_spec((1, V_DIM))],
        out_specs=nblk,
        out_shape=jax.ShapeDtypeStruct((B, S, W), BF16),
        compiler_params=_params(("arbitrary",)),
        name="attn_sample",
    )(q, cache_k, cache_v, k, v, lam_vecs, subln_g.reshape(1, V_DIM))


def _merge_kernel(x_ref, py_ref, at_ref, gmix_ref, wg_ref, bg_ref, wpb_ref, wab_ref, wo_ref,
                  gffn_ref, wr_ref, br_ref, tri_ref,
                  x1_ref, ri_ref, wc_ref, cnt_ref, wrow_ref, run_ref):
    step = pl.program_id(0)
    tm, D = x_ref.shape

    @pl.when(step == 0)
    def _():
        wrow_ref[...] = jnp.zeros_like(wrow_ref)
        run_ref[...] = jnp.zeros_like(run_ref)

    x = x_ref[...]
    h = _rms(x, gmix_ref[...]).astype(BF16)
    gl = jnp.dot(h, wg_ref[...], preferred_element_type=F32) + bg_ref[...]
    a = jnp.dot(py_ref[...], wpb_ref[...], preferred_element_type=F32)
    bb = jnp.dot(at_ref[...], wab_ref[...], preferred_element_type=F32)
    merged = jax.nn.sigmoid(gl[:, :D]) * a + jax.nn.sigmoid(gl[:, D:]) * bb
    x1 = x + jnp.dot(merged.astype(BF16), wo_ref[...], preferred_element_type=F32)
    x1_ref[...] = x1

    h2 = _rms(x1, gffn_ref[...]).astype(BF16)
    logits = jnp.dot(h2, wr_ref[...], preferred_element_type=F32) + br_ref[...]
    lt = logits.T
    G = N_EXPERT_GROUPS
    E = EXPERTS_PER_GROUP
    best = lt[0:1]
    gsel = jnp.zeros((1, tm), jnp.int32)
    for g in range(1, G):
        better = lt[g:g + 1] > best
        gsel = jnp.where(better, g, gsel)
        best = jnp.where(better, lt[g:g + 1], best)
    gden = jnp.zeros((1, tm), F32)
    for g in range(G):
        gden = gden + jnp.exp(lt[g:g + 1] - best)
    g_w = 1.0 / gden
    el = [lt[G + e:G + e + 1] for e in range(E)]
    for g in range(1, G):
        sel = gsel == g
        el = [jnp.where(sel, lt[G + g * E + e:G + g * E + e + 1], el[e]) for e in range(E)]
    v0 = el[0]
    i0 = jnp.zeros((1, tm), jnp.int32)
    for e in range(1, E):
        better = el[e] > v0
        i0 = jnp.where(better, e, i0)
        v0 = jnp.where(better, el[e], v0)
    v1 = jnp.full((1, tm), -jnp.inf, F32)
    i1 = jnp.zeros((1, tm), jnp.int32)
    for e in range(E):
        better = (i0 != e) & (el[e] > v1)
        i1 = jnp.where(better, e, i1)
        v1 = jnp.where(better, el[e], v1)
    ex = jnp.exp(v1 - v0)
    w0 = (1.0 / (1.0 + ex)) * g_w
    w1 = (ex / (1.0 + ex)) * g_w
    e0 = gsel * E + i0
    e1 = gsel * E + i1

    eid = lax.broadcasted_iota(jnp.int32, (N_EXPERTS, tm), 0)
    oh0 = eid == e0
    oh1 = eid == e1
    oh = jnp.where(oh0 | oh1, 1.0, 0.0).astype(BF16)
    pref = jnp.dot(oh, tri_ref[...], preferred_element_type=F32)
    rank = run_ref[...] + pref - 1.0
    r0 = jnp.sum(jnp.where(oh0, rank, 0.0), axis=0, keepdims=True)
    r1 = jnp.sum(jnp.where(oh1, rank, 0.0), axis=0, keepdims=True)
    run_ref[...] = run_ref[...] + pref[:, tm - 1:tm]
    cnt_ref[...] = jnp.broadcast_to(run_ref[...], cnt_ref.shape).astype(jnp.int32)

    ri_ref[0:1, :] = e0
    ri_ref[1:2, :] = e1
    ri_ref[2:3, :] = r0.astype(jnp.int32)
    ri_ref[3:4, :] = r1.astype(jnp.int32)
    wrow_ref[0:1, :] = w0
    wrow_ref[1:2, :] = w1
    wc_ref[...] = wrow_ref[...].T


def _merge(x, py, at, p):
    N, D = x.shape
    tm = min(N, 512)
    assert N % tm == 0
    tri = (jnp.arange(tm)[:, None] <= jnp.arange(tm)[None, :]).astype(BF16)
    row = lambda w: pl.BlockSpec((tm, w), lambda i: (i, 0))
    consts = [p["gmix"], p["w_gate"], p["b_gate"], p["w_pb"], p["w_ab"], p["w_out"],
              p["gffn"], p["w_router"], p["b_router"], tri]
    return pl.pallas_call(
        _merge_kernel,
        grid=(N // tm,),
        in_specs=[row(D), row(py.shape[1]), row(at.shape[1])] + [_const_spec(c.shape) for c in consts],
        out_specs=[row(D), pl.BlockSpec((4, tm), lambda i: (0, i)), row(LANES),
                   _const_spec((N_EXPERTS, LANES))],
        out_shape=[jax.ShapeDtypeStruct((N, D), F32),
                   jax.ShapeDtypeStruct((4, N), jnp.int32),
                   jax.ShapeDtypeStruct((N, LANES), F32),
                   jax.ShapeDtypeStruct((N_EXPERTS, LANES), jnp.int32)],
        scratch_shapes=[pltpu.VMEM((LANES, tm), F32), pltpu.VMEM((N_EXPERTS, 1), F32)],
        compiler_params=_params(("arbitrary",)),
        name="merge",
    )(x, py, at, *consts)


def _row_tiles_store(ref_view, x, row0=0):
    rows = x.shape[0]
    n = x.shape[1] // LANES
    for s in range(n):
        ref_view[pl.ds(row0 * n + s, rows, stride=n), :] = x[:, s * LANES:(s + 1) * LANES]


def _row_tiles_load(ref_view, rows, width, row0=0):
    n = width // LANES
    return jnp.concatenate([ref_view[pl.ds(row0 * n + s, rows, stride=n), :] for s in range(n)], axis=1)


def _dispatch_kernel(pos_ref, x1_ref, g_ref, xs_ref, buf_ref, sem_ref):
    i = pl.program_id(0)
    n = pl.num_programs(0)
    td, D = x1_ref.shape
    rt = D // LANES
    slot = i % 2

    def drain(s):
        for _ in range(2):
            pltpu.make_async_copy(buf_ref.at[s], xs_ref.at[pl.ds(0, td * rt)], sem_ref.at[s]).wait()

    @pl.when(i >= 2)
    def _():
        drain(slot)

    _row_tiles_store(buf_ref.at[slot], _rms(x1_ref[...], g_ref[...]))

    def issue(r8, _):
        for u in range(ISSUE_UNROLL):
            r = r8 * ISSUE_UNROLL + u
            for c in range(2):
                dst = pl.multiple_of(pos_ref[0, c * td + r] * rt, rt)
                pltpu.make_async_copy(buf_ref.at[slot, pl.ds(r * rt, rt)], xs_ref.at[pl.ds(dst, rt)],
                                      sem_ref.at[slot]).start(priority=c)
        return 0

    lax.fori_loop(0, td // ISSUE_UNROLL, issue, 0)

    @pl.when(i == n - 1)
    def _():
        drain(slot)

        @pl.when(n >= 2)
        def _():
            drain(1 - slot)


def _dispatch(x1, pos, gffn):
    N, D = x1.shape
    td = min(N, 1024)
    assert N % td == 0 and td % ISSUE_UNROLL == 0
    nt = N // td
    rt = D // LANES
    pos3 = pos.reshape(2, nt, td).transpose(1, 0, 2).reshape(nt, 1, 2 * td)
    return pl.pallas_call(
        _dispatch_kernel,
        grid=(nt,),
        in_specs=[pl.BlockSpec((None, 1, 2 * td), lambda i: (i, 0, 0), memory_space=pltpu.SMEM),
                  pl.BlockSpec((td, D), lambda i: (i, 0)), _const_spec((1, D))],
        out_specs=pl.BlockSpec(memory_space=pl.ANY),
        out_shape=jax.ShapeDtypeStruct((2 * N * rt, LANES), F32),
        scratch_shapes=[pltpu.VMEM((2, td * rt, LANES), F32), pltpu.SemaphoreType.DMA((2,))],
        compiler_params=_params(("arbitrary",)),
        name="dispatch",
    )(pos3, x1, gffn)


def _ffn_kernel(tile_ref, ex_ref, lo_ref, hi_ref, first_ref, xs_ref, wg_ref, wu_ref, wd_ref, ys_ref):
    i = pl.program_id(0)
    D = wg_ref.shape[0]
    rt = D // LANES
    tf = xs_ref.shape[0] // rt
    lo = lo_ref[i]
    hi = hi_ref[i]

    @pl.when(first_ref[i] == 1)
    def _():
        ys_ref[...] = jnp.zeros_like(ys_ref)

    @pl.when(hi > lo)
    def _():
        rc = tf // FFN_ROW_CHUNKS
        for ch in range(FFN_ROW_CHUNKS):
            r0 = ch * rc
            xb = _row_tiles_load(xs_ref, rc, D, r0).astype(BF16)
            gate = jnp.dot(xb, wg_ref[...], preferred_element_type=F32)
            up = jnp.dot(xb, wu_ref[...], preferred_element_type=F32)
            hid = (jax.nn.silu(gate) * up).astype(BF16)
            y = jnp.dot(hid, wd_ref[...], preferred_element_type=F32)
            rows = tile_ref[i] * tf + r0 + lax.broadcasted_iota(jnp.int32, (rc, 1), 0)
            mine = (rows >= lo) & (rows < hi)
            _row_tiles_store(ys_ref, jnp.where(mine, y, _row_tiles_load(ys_ref, rc, D, r0)), r0)


def _ffn(xs, sched, w_gate, w_up, w_down, tf):
    D, Hd = w_gate.shape[1:]
    rt = D // LANES
    n_items = sched[0].shape[0]
    rows = pl.BlockSpec((tf * rt, LANES), lambda i, tile, ex, lo, hi, first: (tile[i], 0))
    wspec = lambda a, b_: pl.BlockSpec((None, a, b_), lambda i, tile, ex, lo, hi, first: (ex[i], 0, 0))
    return pl.pallas_call(
        _ffn_kernel,
        grid_spec=pltpu.PrefetchScalarGridSpec(
            num_scalar_prefetch=5,
            grid=(n_items,),
            in_specs=[rows, wspec(D, Hd), wspec(D, Hd), wspec(Hd, D)],
            out_specs=rows,
        ),
        out_shape=jax.ShapeDtypeStruct(xs.shape, F32),
        compiler_params=_params(("arbitrary",)),
        name="expert_ffn",
    )(*sched, xs, w_gate, w_up, w_down)


def _schedule(counts, M, tf):
    n_tiles = M // tf
    n_items = n_tiles + N_EXPERTS - 1
    ends = jnp.cumsum(counts)
    starts = ends - counts
    first_tile = starts // tf
    last_tile = jnp.maximum(ends - 1, 0) // tf
    per = jnp.where(counts > 0, last_tile - first_tile + 1, 0)
    item_end = jnp.cumsum(per)
    item_start = item_end - per
    total = item_end[-1]
    idx = jnp.arange(n_items, dtype=jnp.int32)
    valid = idx < total
    idx_c = jnp.minimum(idx, total - 1)
    ex = jnp.minimum(jnp.sum(item_end[None, :] <= idx_c[:, None], axis=1), N_EXPERTS - 1).astype(jnp.int32)
    onehot = ex[:, None] == jnp.arange(N_EXPERTS)[None, :]
    pick = lambda table: jnp.sum(jnp.where(onehot, table[None, :], 0), axis=1)
    tile = (pick(first_tile) + idx_c - pick(item_start)).astype(jnp.int32)
    lo = jnp.where(valid, pick(starts), 0).astype(jnp.int32)
    hi = jnp.where(valid, pick(ends), 0).astype(jnp.int32)
    prev = jnp.concatenate([jnp.full((1,), -1, jnp.int32), tile[:-1]])
    first = (valid & (tile != prev)).astype(jnp.int32)
    return (tile, ex, lo, hi, first), starts


def _combine_kernel(pos_ref, nxt_ref, x1_ref, wc_ref, g_ref, ys_ref, o_ref, buf_ref, sem_ref):
    i = pl.program_id(0)
    n = pl.num_programs(0)
    tc, D = x1_ref.shape
    rt = D // LANES
    slot = i % 2

    def gather(idx_ref, s):
        def issue(r8, _):
            for u in range(ISSUE_UNROLL):
                r = r8 * ISSUE_UNROLL + u
                for c in range(2):
                    src = pl.multiple_of(idx_ref[0, c * tc + r] * rt, rt)
                    pltpu.make_async_copy(ys_ref.at[pl.ds(src, rt)], buf_ref.at[s, c, pl.ds(r * rt, rt)],
                                          sem_ref.at[s]).start(priority=c)
            return 0
        lax.fori_loop(0, tc // ISSUE_UNROLL, issue, 0)

    @pl.when(i == 0)
    def _():
        gather(pos_ref, slot)

    @pl.when(i + 1 < n)
    def _():
        gather(nxt_ref, 1 - slot)

    for c in range(2):
        pltpu.make_async_copy(ys_ref.at[pl.ds(0, tc * rt)], buf_ref.at[slot, c], sem_ref.at[slot]).wait()

    wc = wc_ref[...]
    r0 = _row_tiles_load(buf_ref.at[slot, 0], tc, D)
    r1 = _row_tiles_load(buf_ref.at[slot, 1], tc, D)
    y = x1_ref[...] + (wc[:, 0:1] * r0 + wc[:, 1:2] * r1)
    o_ref[...] = _rms(y, g_ref[...])


def _combine(x1, pos, wcol, ys, final_g):
    N, D = x1.shape
    tc = min(N, 512)
    assert N % tc == 0 and tc % ISSUE_UNROLL == 0
    nt = N // tc
    rt = D // LANES
    pos3 = pos.reshape(2, nt, tc).transpose(1, 0, 2).reshape(nt, 1, 2 * tc)
    smem = lambda f: pl.BlockSpec((None, 1, 2 * tc), f, memory_space=pltpu.SMEM)
    return pl.pallas_call(
        _combine_kernel,
        grid=(nt,),
        in_specs=[smem(lambda i: (i, 0, 0)), smem(lambda i: (jnp.minimum(i + 1, nt - 1), 0, 0)),
                  pl.BlockSpec((tc, D), lambda i: (i, 0)), pl.BlockSpec((tc, LANES), lambda i: (i, 0)),
                  _const_spec((1, D)), pl.BlockSpec(memory_space=pl.ANY)],
        out_specs=pl.BlockSpec((tc, D), lambda i: (i, 0)),
        out_shape=jax.ShapeDtypeStruct((N, D), F32),
        scratch_shapes=[pltpu.VMEM((2, 2, tc * rt, LANES), F32), pltpu.SemaphoreType.DMA((2,))],
        compiler_params=_params(("arbitrary",)),
        name="combine",
    )(pos3, pos3, x1, wcol, final_g, ys)


def _moe_and_norm(x1, ri, wcol, counts, p):
    N, D = x1.shape
    tf = min(2 * N, 512)
    sched, starts = _schedule(counts, 2 * N, tf)
    hit = ri[0:2][:, None, :] == jnp.arange(N_EXPERTS, dtype=jnp.int32)[None, :, None]
    pos = jnp.sum(jnp.where(hit, starts.astype(jnp.int32)[None, :, None], 0), axis=1) + ri[2:4]
    xs = _dispatch(x1, pos, p["gffn"])
    ys = _ffn(xs, sched, p["w_eg"], p["w_eu"], p["w_ed"], tf)
    return _combine(x1, pos, wcol, ys, p["final_g"])


def _stream(x, pos0, pool_hdr, cache, p, n_heads):
    B, T, D = x.shape
    q, k, v, py, npool = _inproj(x, pool_hdr, pos0, p["gmix"], p["w_a"], p["pool_w"], p["pool_scale"], n_heads)
    if cache is None:
        at = _attn_prompt(q, k, v, p["lam_vecs"], p["subln_g"], n_heads)
    else:
        at = _attn_sample(q, k, v, cache[0], cache[1], p["lam_vecs"], p["subln_g"], n_heads)
    N = B * T
    x1, ri, wcol, cnt = _merge(x.reshape(N, D), py.reshape(N, -1), at.reshape(N, -1), p)
    y = _moe_and_norm(x1, ri, wcol, cnt[:, 0], p)
    return (y.reshape(B, T, D), k.reshape(1, B, T, n_heads, 2 * HEAD_DIM), v.reshape(1, B, T, n_heads, V_DIM),
            npool[None, :, 1:, :])


def kernel(x_prompt, x_sample, cache_k, cache_v, state_pool, norm_mix_g, w_in, b_gate, lambda_q1, lambda_k1,
           lambda_q2, lambda_k2, subln_g, pool_w, pool_scale, w_pool_branch, w_attn_branch, w_out, norm_ffn_g,
           w_router_group, b_router_group, w_router_expert, b_router_expert, w_expert_gate, w_expert_up,
           w_expert_down, final_norm_g):
    assert w_in.shape[0] == 1, "single layer"
    D = x_prompt.shape[-1]
    n_heads = cache_k.shape[3]
    P = pool_scale.shape[-1]
    past = cache_k.shape[2]
    n_a = P + 2 * n_heads * 2 * HEAD_DIM + n_heads * V_DIM

    wr = jnp.concatenate([w_router_group[0], jnp.transpose(w_router_expert[0], (1, 0, 2)).reshape(D, N_EXPERTS)],
                         axis=1)
    br = jnp.concatenate([b_router_group[0], b_router_expert[0].reshape(N_EXPERTS)])
    n_r = wr.shape[1]
    p = dict(
        gmix=norm_mix_g[0].reshape(1, D),
        w_a=w_in[0, :, :n_a].astype(BF16),
        w_gate=w_in[0, :, n_a:].astype(BF16),
        b_gate=b_gate[0].reshape(1, 2 * D),
        lam_vecs=jnp.stack([lambda_q1[0], lambda_k1[0], lambda_q2[0], lambda_k2[0]]),
        subln_g=subln_g[0],
        pool_w=pool_w[0].astype(BF16),
        pool_scale=pool_scale[0],
        w_pb=w_pool_branch[0].astype(BF16),
        w_ab=w_attn_branch[0].astype(BF16),
        w_out=w_out[0].astype(BF16),
        gffn=norm_ffn_g[0].reshape(1, D),
        w_router=jnp.pad(wr, ((0, 0), (0, LANES - n_r))).astype(BF16),
        b_router=jnp.pad(br, (0, LANES - n_r)).reshape(1, LANES),
        w_eg=w_expert_gate[0].astype(BF16),
        w_eu=w_expert_up[0].astype(BF16),
        w_ed=w_expert_down[0].astype(BF16),
        final_g=final_norm_g.reshape(1, D),
    )

    Bp = x_prompt.shape[0]
    Bs = x_sample.shape[0]
    hdr_p = jnp.zeros((Bp, POOL_HDR, P), F32)
    hdr_s = jnp.pad(state_pool[0], ((0, 0), (1, 0), (0, 0)))
    cache = (cache_k[0].reshape(Bs, past * n_heads, -1), cache_v[0].reshape(Bs, past * n_heads, -1))

    yp, kp, vp, pp = _stream(x_prompt, 0, hdr_p, None, p, n_heads)
    ys, ks, vs, ps = _stream(x_sample, past, hdr_s, cache, p, n_heads)
    return (yp, ys, kp, vp, pp, ks, vs, ps)
```

```python
import functools
import math

import jax
import jax.numpy as jnp
from jax import lax
from jax.experimental import pallas as pl
from jax.experimental.pallas import tpu as pltpu

CHUNK = 64
HEAD_DIM = 64
V_DIM = 2 * HEAD_DIM
ROT_DIM = HEAD_DIM // 4
ROPE_THETA = 500000.0
ATTN_SCALE = HEAD_DIM ** -0.5
Q_SCALE = ATTN_SCALE * math.log2(math.e)
POOL_WINDOWS = (2, 4, 8, 16)
POOL_GROUP_DIM = 128
POOL_BUF = max(POOL_WINDOWS) - 1
POOL_HDR = POOL_BUF + 1
N_EXPERT_GROUPS = 4
EXPERTS_PER_GROUP = 4
N_EXPERTS = N_EXPERT_GROUPS * EXPERTS_PER_GROUP
PAIRS_PER_GROUP = EXPERTS_PER_GROUP * (EXPERTS_PER_GROUP - 1) // 2
N_CLASSES = N_EXPERT_GROUPS * PAIRS_PER_GROUP
CLASS_ROWS = 32
PAIR_LO = (0, 0, 0, 1, 1, 2)
PAIR_HI = (1, 2, 3, 2, 3, 3)
assert EXPERTS_PER_GROUP == 4 and N_CLASSES <= CLASS_ROWS
RMS_EPS = 1e-6
LAM_INIT = 0.8 - 0.6 * math.exp(-0.3 * 0)

LANES = 128
ISSUE_UNROLL = 8
ONES_ROWS = 16
FFN_ROW_CHUNKS = 2
VMEM_LIMIT = 56 * 1024 * 1024
NEG_BIG = -1e30

F32 = jnp.float32
BF16 = jnp.bfloat16


def _params(semantics):
    return pltpu.CompilerParams(dimension_semantics=semantics, vmem_limit_bytes=VMEM_LIMIT)


def _const_spec(shape):
    nd = len(shape)
    return pl.BlockSpec(shape, lambda *_: (0,) * nd)


def _rms(x, g):
    ms = jnp.mean(x * x, axis=-1, keepdims=True)
    return (x * lax.rsqrt(ms + RMS_EPS)) * g


def _inproj_kernel(x_ref, g_ref, w_ref, cos_ref, sa_ref, sb_ref, pbuf_ref, pw_ref, ps_ref,
                   q_ref, k_ref, v_ref, py_ref, npool_ref, ext_ref, carry_ref, *, pos0, n_heads):
    t = pl.program_id(0)
    b = pl.program_id(1)
    tm = x_ref.shape[0]
    pool_w = ext_ref.shape[1]
    qk_w = n_heads * 2 * HEAD_DIM

    h = _rms(x_ref[...], g_ref[...]).astype(BF16)

    @pl.when(t == 0)
    def _():
        ext_ref[0:POOL_HDR, :] = pbuf_ref[...]

    @pl.when(t > 0)
    def _():
        ext_ref[0:POOL_HDR, :] = carry_ref[b]

    ext_ref[POOL_HDR:, :] = jnp.dot(h, w_ref[:, 0:pool_w], preferred_element_type=F32)
    tail = ext_ref[tm:tm + POOL_HDR, :]
    carry_ref[b] = tail
    npool_ref[...] = tail

    cos = cos_ref[...]
    sa = sa_ref[...]
    sb = sb_ref[...]
    col_chunk = 4 * LANES
    for which, out_ref in ((0, q_ref), (1, k_ref)):
        for c in range(0, qk_w, col_chunk):
            base = pool_w + which * qk_w + c
            z = jnp.dot(h, w_ref[:, base:base + col_chunk], preferred_element_type=F32)
            for j in range(0, col_chunk, LANES):
                zc = z[:, j:j + LANES]
                rot = (zc * cos + pltpu.roll(zc, LANES - ROT_DIM // 2, 1) * sa
                       + pltpu.roll(zc, ROT_DIM // 2, 1) * sb)
                if which == 0:
                    out_ref[:, c + j:c + j + LANES] = (rot * Q_SCALE).astype(BF16)
                else:
                    out_ref[:, c + j:c + j + LANES] = rot

    v_w = v_ref.shape[1]
    for c in range(0, v_w, col_chunk):
        base = pool_w + 2 * qk_w + c
        v_ref[:, c:c + col_chunk] = jnp.dot(h, w_ref[:, base:base + col_chunk], preferred_element_type=F32)

    pos = pos0 + t * tm + lax.broadcasted_iota(jnp.int32, (tm, 1), 0)
    for g, w in enumerate(POOL_WINDOWS):
        sl = slice(g * POOL_GROUP_DIM, (g + 1) * POOL_GROUP_DIM)
        u = ext_ref[POOL_HDR:POOL_HDR + tm, sl]
        ws = u
        for j in range(1, w):
            ws = ws + ext_ref[POOL_HDR - j:POOL_HDR - j + tm, sl]
        cnt = jnp.minimum(w, pos + 1).astype(F32)
        d = (ws / cnt - u).astype(BF16)
        y = jnp.dot(d, pw_ref[g], preferred_element_type=F32) * ps_ref[:, sl]
        py_ref[:, sl] = y.astype(BF16)


def _inproj(x, pool_hdr, pos0, norm_g, w_a, pool_w, pool_scale, n_heads):
    B, T, D = x.shape
    P = pool_scale.shape[-1]
    qk_w = n_heads * 2 * HEAD_DIM
    v_w = n_heads * V_DIM
    tm = min(T, 512)
    assert T % tm == 0 and T >= POOL_HDR and tm % 16 == 0
    nt = T // tm

    pos = pos0 + jnp.arange(T)
    inv_freq = jnp.power(ROPE_THETA, -jnp.arange(0, ROT_DIM, 2, dtype=F32) / ROT_DIM)
    ang = pos.astype(F32)[:, None] * inv_freq[None, :]
    cos, sin = jnp.cos(ang), jnp.sin(ang)
    half = ROT_DIM // 2
    ones = jnp.ones((T, HEAD_DIM - ROT_DIM), F32)
    zeros_h = jnp.zeros((T, half), F32)
    zeros_r = jnp.zeros((T, HEAD_DIM - ROT_DIM), F32)
    cos_t = jnp.tile(jnp.concatenate([cos, cos, ones], axis=1), (1, LANES // HEAD_DIM))
    sa_t = jnp.tile(jnp.concatenate([-sin, zeros_h, zeros_r], axis=1), (1, LANES // HEAD_DIM))
    sb_t = jnp.tile(jnp.concatenate([zeros_h, sin, zeros_r], axis=1), (1, LANES // HEAD_DIM))

    row = lambda w: pl.BlockSpec((None, tm, w), lambda t, b: (b, t, 0))
    tab = pl.BlockSpec((tm, LANES), lambda t, b: (t, 0))
    hdr = pl.BlockSpec((None, POOL_HDR, P), lambda t, b: (b, 0, 0))
    hdr_out = pl.BlockSpec((None, POOL_HDR, P), lambda t, b: (jnp.where(t == nt - 1, b, 0), 0, 0))
    kern = functools.partial(_inproj_kernel, pos0=pos0, n_heads=n_heads)
    return pl.pallas_call(
        kern,
        grid=(nt, B),
        in_specs=[row(D), _const_spec((1, D)), _const_spec(w_a.shape), tab, tab, tab, hdr,
                  _const_spec(pool_w.shape), _const_spec((1, P))],
        out_specs=[row(qk_w), row(qk_w), row(v_w), row(P), hdr_out],
        out_shape=[jax.ShapeDtypeStruct((B, T, qk_w), BF16),
                   jax.ShapeDtypeStruct((B, T, qk_w), F32),
                   jax.ShapeDtypeStruct((B, T, v_w), F32),
                   jax.ShapeDtypeStruct((B, T, P), BF16),
                   jax.ShapeDtypeStruct((B, POOL_HDR, P), F32)],
        scratch_shapes=[pltpu.VMEM((POOL_HDR + tm, P), F32), pltpu.VMEM((B, POOL_HDR, P), F32)],
        compiler_params=_params(("arbitrary", "arbitrary")),
        name="inproj",
    )(x, norm_g.reshape(1, D), w_a, cos_t, sa_t, sb_t, pool_hdr, pool_w, pool_scale.reshape(1, P))


def _lambda(lam_ref):
    lv = lam_ref[...]
    a = jnp.sum(lv[0:1] * lv[1:2], axis=-1, keepdims=True)
    c = jnp.sum(lv[2:3] * lv[3:4], axis=-1, keepdims=True)
    return jnp.exp(a) - jnp.exp(c) + LAM_INIT


def _split_maps(q):
    lane = lax.broadcasted_iota(jnp.int32, q.shape, 1)
    zero = jnp.zeros_like(q)
    return jnp.where(lane < HEAD_DIM, q, zero), jnp.where(lane >= HEAD_DIM, q, zero)


def _nt_dot(a, b):
    return lax.dot_general(a, b, (((1,), (1,)), ((), ())), preferred_element_type=F32)


def _head_out(o, g):
    ms = jnp.mean(o * o, axis=-1, keepdims=True)
    return ((o * lax.rsqrt(ms + RMS_EPS)) * g) * (1.0 - LAM_INIT)


def _attn_prompt_kernel(q_ref, k_ref, v_ref, lam_ref, g_ref, o_ref,
                        kb_ref, vt_ref, qt_ref, acc1_ref, acc2_ref, stat_ref, sa_ref, sb_ref, sc_ref, *, tq, tk):
    T = q_ref.shape[0]
    nq = T // tq
    half = tq // 2
    assert tk == half

    kb_ref[...] = k_ref[...].astype(BF16)
    ones_row = jnp.where(lax.broadcasted_iota(jnp.int32, (ONES_ROWS, tk), 0) == 0, 1.0, 0.0).astype(BF16)
    for c in range(T // tk):
        vt_ref[c, 0:V_DIM, :] = v_ref[c * tk:(c + 1) * tk, :].T.astype(BF16)
        vt_ref[c, V_DIM:, :] = ones_row
    for c in range(nq):
        qt = q_ref[c * tq:(c + 1) * tq, :].astype(F32).T.astype(BF16)
        row = lax.broadcasted_iota(jnp.int32, qt.shape, 0)
        zero = jnp.zeros_like(qt)
        qt_ref[c, 0] = jnp.where(row < HEAD_DIM, qt, zero)
        qt_ref[c, 1] = jnp.where(row >= HEAD_DIM, qt, zero)

    def scores(j, s_ref, i, right_half=False):
        kj = kb_ref[pl.ds(pl.multiple_of(j * tk, tk), tk), :]
        for c in range(2):
            if right_half:
                s_ref[c, :, half:] = jnp.dot(kj, qt_ref[i, c, :, half:], preferred_element_type=F32)
            else:
                s_ref[c] = jnp.dot(kj, qt_ref[i, c], preferred_element_type=F32)

    def softmax_pv(j, s_ref, kind="full", first=False):
        vtj = vt_ref[j]
        cols = slice(half, tq) if kind == "diag2" else slice(0, tq)
        width = cols.stop - cols.start
        if kind != "full":
            key_chunk = lax.broadcasted_iota(jnp.int32, (tk, width), 0) // CHUNK
            qry_chunk = lax.broadcasted_iota(jnp.int32, (tk, width), 1) // CHUNK
            mask = key_chunk <= qry_chunk
        for c, acc_ref in ((0, acc1_ref), (1, acc2_ref)):
            s = s_ref[c, :, cols]
            if kind != "full":
                s = jnp.where(mask, s, NEG_BIG)
            smax = jnp.max(s, axis=0, keepdims=True)
            if first:
                mn = smax
            else:
                m = stat_ref[c:c + 1, cols]
                mn = jnp.maximum(m, smax)
                alpha = jnp.exp2(m - mn)
            p = jnp.exp2(s - mn)
            stat_ref[c:c + 1, cols] = mn
            pv = jnp.dot(vtj, p.astype(BF16), preferred_element_type=F32)
            if first:
                acc_ref[:, cols] = pv
            else:
                acc_ref[:, cols] = alpha * acc_ref[:, cols] + pv

    lam = _lambda(lam_ref)
    scores(0, sc_ref, 0)

    def q_tile(i, _):
        nxt = jnp.minimum(i + 1, nq - 1)

        @pl.when(i == 0)
        def _():
            scores(1, sa_ref, i, right_half=True)
            softmax_pv(0, sc_ref, "diag1", first=True)
            scores(0, sc_ref, nxt)
            softmax_pv(1, sa_ref, "diag2")

        @pl.when(i > 0)
        def _():
            scores(1, sa_ref, i)
            softmax_pv(0, sc_ref, "full", first=True)

            def pair(p, _):
                t = 2 * p + 1
                scores(t + 1, sb_ref, i)
                softmax_pv(t, sa_ref)
                scores(t + 2, sa_ref, i)
                softmax_pv(t + 1, sb_ref)
                return 0

            lax.fori_loop(0, i - 1, pair, 0)

            d1 = 2 * i
            scores(d1, sb_ref, i)
            softmax_pv(d1 - 1, sa_ref)
            scores(d1 + 1, sa_ref, i, right_half=True)
            softmax_pv(d1, sb_ref, "diag1")
            scores(0, sc_ref, nxt)
            softmax_pv(d1 + 1, sa_ref, "diag2")

        l1 = acc1_ref[V_DIM:V_DIM + 1, :]
        l2 = acc2_ref[V_DIM:V_DIM + 1, :]
        o = acc1_ref[0:V_DIM, :] / l1 - lam * (acc2_ref[0:V_DIM, :] / l2)
        ms = jnp.mean(o * o, axis=0, keepdims=True)
        o = ((o * lax.rsqrt(ms + RMS_EPS)) * g_ref[...]) * (1.0 - LAM_INIT)
        o_ref[pl.ds(pl.multiple_of(i * tq, tq), tq), :] = o.T.astype(BF16)
        return 0

    lax.fori_loop(0, nq, q_tile, 0)


def _attn_prompt(q, k, v, lam_vecs, subln_g, n_heads):
    B, T, _ = q.shape
    tq = min(T, 1024)
    tk = tq // 2
    assert T % tq == 0 and tk % LANES == 0 and tk % CHUNK == 0
    blk = pl.BlockSpec((None, T, V_DIM), lambda b, h: (b, 0, h))
    score_buf = pltpu.VMEM((2, tk, tq), F32)
    acc_buf = pltpu.VMEM((V_DIM + ONES_ROWS, tq), F32)
    return pl.pallas_call(
        functools.partial(_attn_prompt_kernel, tq=tq, tk=tk),
        grid=(B, n_heads),
        in_specs=[blk, blk, blk, _const_spec(lam_vecs.shape), _const_spec((V_DIM, 1))],
        out_specs=blk,
        out_shape=jax.ShapeDtypeStruct((B, T, n_heads * V_DIM), BF16),
        scratch_shapes=[pltpu.VMEM((T, V_DIM), BF16), pltpu.VMEM((T // tk, V_DIM + ONES_ROWS, tk), BF16),
                        pltpu.VMEM((T // tq, 2, V_DIM, tq), BF16), acc_buf, acc_buf,
                        pltpu.VMEM((2, tq), F32), score_buf, score_buf, score_buf],
        compiler_params=_params(("arbitrary", "arbitrary")),
        name="attn_prompt",
    )(q, k, v, lam_vecs, subln_g.reshape(V_DIM, 1))


def _attn_sample_kernel(q_ref, ck_ref, cv_ref, nk_ref, nv_ref, lam_ref, g_ref, o_ref, *, past, n_heads):
    S = q_ref.shape[0]
    P = past
    q_chunk = (past + lax.broadcasted_iota(jnp.int32, (S, 1), 0)) // CHUNK
    mask_c = (lax.broadcasted_iota(jnp.int32, (S, P), 1) // CHUNK) <= q_chunk
    mask_n = ((past + lax.broadcasted_iota(jnp.int32, (S, S), 1)) // CHUNK) <= q_chunk
    lam = _lambda(lam_ref)
    for h in range(n_heads):
        cols = slice(h * V_DIM, (h + 1) * V_DIM)
        q1, q2 = _split_maps(q_ref[:, cols])
        ck = ck_ref[pl.ds(h, P, stride=n_heads), :].astype(BF16)
        cv = cv_ref[pl.ds(h, P, stride=n_heads), :].astype(BF16)
        nk = nk_ref[:, cols].astype(BF16)
        nv = nv_ref[:, cols].astype(BF16)
        outs = []
        for qm in (q1, q2):
            sc = jnp.where(mask_c, _nt_dot(qm, ck), NEG_BIG)
            sn = jnp.where(mask_n, _nt_dot(qm, nk), NEG_BIG)
            m = jnp.maximum(jnp.max(sc, axis=-1, keepdims=True), jnp.max(sn, axis=-1, keepdims=True))
            pc = jnp.exp2(sc - m)
            pn = jnp.exp2(sn - m)
            l = jnp.sum(pc, axis=-1, keepdims=True) + jnp.sum(pn, axis=-1, keepdims=True)
            acc = (jnp.dot(pc.astype(BF16), cv, preferred_element_type=F32)
                   + jnp.dot(pn.astype(BF16), nv, preferred_element_type=F32))
            outs.append(acc / l)
        o = outs[0] - lam * outs[1]
        o_ref[:, cols] = _head_out(o, g_ref[...]).astype(BF16)


def _attn_sample(q, k, v, cache_k, cache_v, lam_vecs, subln_g, n_heads):
    B, S, W = q.shape
    P = cache_k.shape[1] // n_heads
    nblk = pl.BlockSpec((None, S, W), lambda b: (b, 0, 0))
    cblk = pl.BlockSpec((None, P * n_heads, V_DIM), lambda b: (b, 0, 0))
    kern = functools.partial(_attn_sample_kernel, past=P, n_heads=n_heads)
    return pl.pallas_call(
        kern,
        grid=(B,),
        in_specs=[nblk, cblk, cblk, nblk, nblk, _const_spec(lam_vecs.shape), _const_spec((1, V_DIM))],
        out_specs=nblk,
        out_shape=jax.ShapeDtypeStruct((B, S, W), BF16),
        compiler_params=_params(("arbitrary",)),
        name="attn_sample",
    )(q, cache_k, cache_v, k, v, lam_vecs, subln_g.reshape(1, V_DIM))


def _merge_kernel(x_ref, py_ref, at_ref, gmix_ref, wg_ref, bg_ref, wpb_ref, wab_ref, wo_ref,
                  gffn_ref, wr_ref, br_ref, tri_ref,
                  x1_ref, ri_ref, wc_ref, cnt_ref, wrow_ref, run_ref):
    step = pl.program_id(0)
    tm, D = x_ref.shape

    @pl.when(step == 0)
    def _():
        wrow_ref[...] = jnp.zeros_like(wrow_ref)
        run_ref[...] = jnp.zeros_like(run_ref)

    x = x_ref[...]
    h = _rms(x, gmix_ref[...]).astype(BF16)
    gl = jnp.dot(h, wg_ref[...], preferred_element_type=F32) + bg_ref[...]
    a = jnp.dot(py_ref[...], wpb_ref[...], preferred_element_type=F32)
    bb = jnp.dot(at_ref[...], wab_ref[...], preferred_element_type=F32)
    merged = jax.nn.sigmoid(gl[:, :D]) * a + jax.nn.sigmoid(gl[:, D:]) * bb
    x1 = x + jnp.dot(merged.astype(BF16), wo_ref[...], preferred_element_type=F32)
    x1_ref[...] = x1

    h2 = _rms(x1, gffn_ref[...]).astype(BF16)
    logits = jnp.dot(h2, wr_ref[...], preferred_element_type=F32) + br_ref[...]
    lt = logits.T
    G = N_EXPERT_GROUPS
    E = EXPERTS_PER_GROUP
    best = lt[0:1]
    gsel = jnp.zeros((1, tm), jnp.int32)
    for g in range(1, G):
        better = lt[g:g + 1] > best
        gsel = jnp.where(better, g, gsel)
        best = jnp.where(better, lt[g:g + 1], best)
    gden = jnp.zeros((1, tm), F32)
    for g in range(G):
        gden = gden + jnp.exp(lt[g:g + 1] - best)
    g_w = 1.0 / gden
    el = [lt[G + e:G + e + 1] for e in range(E)]
    for g in range(1, G):
        sel = gsel == g
        el = [jnp.where(sel, lt[G + g * E + e:G + g * E + e + 1], el[e]) for e in range(E)]
    v0 = el[0]
    i0 = jnp.zeros((1, tm), jnp.int32)
    for e in range(1, E):
        better = el[e] > v0
        i0 = jnp.where(better, e, i0)
        v0 = jnp.where(better, el[e], v0)
    v1 = jnp.full((1, tm), -jnp.inf, F32)
    i1 = jnp.zeros((1, tm), jnp.int32)
    for e in range(E):
        better = (i0 != e) & (el[e] > v1)
        i1 = jnp.where(better, e, i1)
        v1 = jnp.where(better, el[e], v1)
    ex = jnp.exp(v1 - v0)
    w0 = (1.0 / (1.0 + ex)) * g_w
    w1 = (ex / (1.0 + ex)) * g_w

    swap = i0 > i1
    lo = jnp.where(swap, i1, i0)
    hi = jnp.where(swap, i0, i1)
    pair = jnp.where(lo == 0, 0, jnp.where(lo == 1, E - 1, 2 * E - 3)) + (hi - lo - 1)
    cls = gsel * PAIRS_PER_GROUP + pair
    w_lo = jnp.where(swap, w1, w0)
    w_hi = jnp.where(swap, w0, w1)

    cid = lax.broadcasted_iota(jnp.int32, (CLASS_ROWS, tm), 0)
    oh = cid == cls
    pref = jnp.dot(jnp.where(oh, 1.0, 0.0).astype(BF16), tri_ref[...], preferred_element_type=F32)
    rank = jnp.sum(jnp.where(oh, run_ref[...] + pref - 1.0, 0.0), axis=0, keepdims=True)
    run_ref[...] = run_ref[...] + pref[:, tm - 1:tm]
    cnt_ref[...] = jnp.broadcast_to(run_ref[...], cnt_ref.shape).astype(jnp.int32)

    ri_ref[0:1, :] = cls
    ri_ref[1:2, :] = rank.astype(jnp.int32)
    wrow_ref[0:1, :] = w_lo
    wrow_ref[1:2, :] = w_hi
    wc_ref[...] = wrow_ref[...].T


def _merge(x, py, at, p):
    N, D = x.shape
    tm = min(N, 512)
    assert N % tm == 0
    tri = (jnp.arange(tm)[:, None] <= jnp.arange(tm)[None, :]).astype(BF16)
    row = lambda w: pl.BlockSpec((tm, w), lambda i: (i, 0))
    consts = [p["gmix"], p["w_gate"], p["b_gate"], p["w_pb"], p["w_ab"], p["w_out"],
              p["gffn"], p["w_router"], p["b_router"], tri]
    return pl.pallas_call(
        _merge_kernel,
        grid=(N // tm,),
        in_specs=[row(D), row(py.shape[1]), row(at.shape[1])] + [_const_spec(c.shape) for c in consts],
        out_specs=[row(D), pl.BlockSpec((2, tm), lambda i: (0, i)), row(LANES),
                   _const_spec((CLASS_ROWS, LANES))],
        out_shape=[jax.ShapeDtypeStruct((N, D), F32),
                   jax.ShapeDtypeStruct((2, N), jnp.int32),
                   jax.ShapeDtypeStruct((N, LANES), F32),
                   jax.ShapeDtypeStruct((CLASS_ROWS, LANES), jnp.int32)],
        scratch_shapes=[pltpu.VMEM((LANES, tm), F32), pltpu.VMEM((CLASS_ROWS, 1), F32)],
        compiler_params=_params(("arbitrary",)),
        name="merge",
    )(x, py, at, *consts)


def _row_tiles_store(ref_view, x, row0=0, group=1, member=0):
    rows = x.shape[0]
    n = x.shape[1] // LANES
    for s in range(n):
        ref_view[pl.ds((row0 * group + member) * n + s, rows, stride=n * group), :] = x[:, s * LANES:(s + 1) * LANES]


def _row_tiles_load(ref_view, rows, width, row0=0, group=1, member=0):
    n = width // LANES
    return jnp.concatenate([ref_view[pl.ds((row0 * group + member) * n + s, rows, stride=n * group), :]
                            for s in range(n)], axis=1)


def _dispatch_kernel(pos_ref, x1_ref, g_ref, xs_ref, buf_ref, sem_ref):
    i = pl.program_id(0)
    n = pl.num_programs(0)
    td, D = x1_ref.shape
    rt = D // LANES
    slot = i % 2

    def drain(s):
        pltpu.make_async_copy(buf_ref.at[s], xs_ref.at[pl.ds(0, td * rt)], sem_ref.at[s]).wait()

    @pl.when(i >= 2)
    def _():
        drain(slot)

    _row_tiles_store(buf_ref.at[slot], _rms(x1_ref[...], g_ref[...]))

    def issue(r8, _):
        for u in range(ISSUE_UNROLL):
            r = r8 * ISSUE_UNROLL + u
            dst = pl.multiple_of(pos_ref[0, r] * rt, rt)
            pltpu.make_async_copy(buf_ref.at[slot, pl.ds(r * rt, rt)], xs_ref.at[pl.ds(dst, rt)],
                                  sem_ref.at[slot]).start(priority=u % 2)
        return 0

    lax.fori_loop(0, td // ISSUE_UNROLL, issue, 0)

    @pl.when(i == n - 1)
    def _():
        drain(slot)

        @pl.when(n >= 2)
        def _():
            drain(1 - slot)


def _dispatch(x1, pos, gffn):
    N, D = x1.shape
    td = min(N, 512)
    assert N % td == 0 and td % ISSUE_UNROLL == 0
    nt = N // td
    rt = D // LANES
    pos3 = pos.reshape(nt, 1, td)
    return pl.pallas_call(
        _dispatch_kernel,
        grid=(nt,),
        in_specs=[pl.BlockSpec((None, 1, td), lambda i: (i, 0, 0), memory_space=pltpu.SMEM),
                  pl.BlockSpec((td, D), lambda i: (i, 0)), _const_spec((1, D))],
        out_specs=pl.BlockSpec(memory_space=pl.ANY),
        out_shape=jax.ShapeDtypeStruct((N * rt, LANES), F32),
        scratch_shapes=[pltpu.VMEM((2, td * rt, LANES), F32), pltpu.SemaphoreType.DMA((2,))],
        compiler_params=_params(("arbitrary",)),
        name="dispatch",
    )(pos3, x1, gffn)


def _ffn_kernel(tile_ref, elo_ref, ehi_ref, lo_ref, hi_ref, first_ref, xs_ref,
                wg0_ref, wu0_ref, wd0_ref, wg1_ref, wu1_ref, wd1_ref, ys_ref):
    i = pl.program_id(0)
    D = wg0_ref.shape[0]
    rt = D // LANES
    tf = xs_ref.shape[0] // rt
    lo = lo_ref[i]
    hi = hi_ref[i]

    @pl.when(first_ref[i] == 1)
    def _():
        ys_ref[...] = jnp.zeros_like(ys_ref)

    @pl.when(hi > lo)
    def _():
        rc = tf // FFN_ROW_CHUNKS
        for ch in range(FFN_ROW_CHUNKS):
            r0 = ch * rc
            xb = _row_tiles_load(xs_ref, rc, D, r0).astype(BF16)
            rows = tile_ref[i] * tf + r0 + lax.broadcasted_iota(jnp.int32, (rc, 1), 0)
            mine = (rows >= lo) & (rows < hi)
            for m, (wg_ref, wu_ref, wd_ref) in enumerate(((wg0_ref, wu0_ref, wd0_ref), (wg1_ref, wu1_ref, wd1_ref))):
                gate = jnp.dot(xb, wg_ref[...], preferred_element_type=F32)
                up = jnp.dot(xb, wu_ref[...], preferred_element_type=F32)
                hid = (jax.nn.silu(gate) * up).astype(BF16)
                y = jnp.dot(hid, wd_ref[...], preferred_element_type=F32)
                old = _row_tiles_load(ys_ref, rc, D, r0, group=2, member=m)
                _row_tiles_store(ys_ref, jnp.where(mine, y, old), r0, group=2, member=m)


def _ffn(xs, sched, w_gate, w_up, w_down, tf):
    D, Hd = w_gate.shape[1:]
    rt = D // LANES
    n_items = sched[0].shape[0]
    rows = lambda g: pl.BlockSpec((tf * rt * g, LANES), lambda i, tile, elo, ehi, lo, hi, first: (tile[i], 0))
    w_lo = lambda a, b_: pl.BlockSpec((None, a, b_), lambda i, tile, elo, ehi, lo, hi, first: (elo[i], 0, 0))
    w_hi = lambda a, b_: pl.BlockSpec((None, a, b_), lambda i, tile, elo, ehi, lo, hi, first: (ehi[i], 0, 0))
    return pl.pallas_call(
        _ffn_kernel,
        grid_spec=pltpu.PrefetchScalarGridSpec(
            num_scalar_prefetch=6,
            grid=(n_items,),
            in_specs=[rows(1), w_lo(D, Hd), w_lo(D, Hd), w_lo(Hd, D), w_hi(D, Hd), w_hi(D, Hd), w_hi(Hd, D)],
            out_specs=rows(2),
        ),
        out_shape=jax.ShapeDtypeStruct((2 * xs.shape[0], LANES), F32),
        compiler_params=_params(("arbitrary",)),
        name="expert_ffn",
    )(*sched, xs, w_gate, w_up, w_down, w_gate, w_up, w_down)


def _schedule(counts, M, tf):
    n_tiles = M // tf
    n_items = n_tiles + N_CLASSES - 1
    ends = jnp.cumsum(counts)
    starts = ends - counts
    first_tile = starts // tf
    last_tile = jnp.maximum(ends - 1, 0) // tf
    per = jnp.where(counts > 0, last_tile - first_tile + 1, 0)
    item_end = jnp.cumsum(per)
    item_start = item_end - per
    total = item_end[-1]
    idx = jnp.arange(n_items, dtype=jnp.int32)
    valid = idx < total
    idx_c = jnp.minimum(idx, total - 1)
    cl = jnp.minimum(jnp.sum(item_end[None, :] <= idx_c[:, None], axis=1), N_CLASSES - 1).astype(jnp.int32)
    onehot = cl[:, None] == jnp.arange(N_CLASSES)[None, :]
    pick = lambda table: jnp.sum(jnp.where(onehot, table[None, :], 0), axis=1)
    tile = (pick(first_tile) + idx_c - pick(item_start)).astype(jnp.int32)
    lo = jnp.where(valid, pick(starts), 0).astype(jnp.int32)
    hi = jnp.where(valid, pick(ends), 0).astype(jnp.int32)
    prev = jnp.concatenate([jnp.full((1,), -1, jnp.int32), tile[:-1]])
    first = (valid & (tile != prev)).astype(jnp.int32)
    group = jnp.arange(N_CLASSES, dtype=jnp.int32) // PAIRS_PER_GROUP
    e_lo = group * EXPERTS_PER_GROUP + jnp.tile(jnp.array(PAIR_LO, jnp.int32), N_EXPERT_GROUPS)
    e_hi = group * EXPERTS_PER_GROUP + jnp.tile(jnp.array(PAIR_HI, jnp.int32), N_EXPERT_GROUPS)
    return (tile, pick(e_lo).astype(jnp.int32), pick(e_hi).astype(jnp.int32), lo, hi, first), starts


def _combine_kernel(pos_ref, nxt_ref, x1_ref, wc_ref, g_ref, ys_ref, o_ref, buf_ref, sem_ref):
    i = pl.program_id(0)
    n = pl.num_programs(0)
    tc, D = x1_ref.shape
    rt = D // LANES
    slot = i % 2

    def gather(idx_ref, s):
        def issue(r8, _):
            for u in range(ISSUE_UNROLL):
                r = r8 * ISSUE_UNROLL + u
                src = pl.multiple_of(idx_ref[0, r] * (2 * rt), 2 * rt)
                pltpu.make_async_copy(ys_ref.at[pl.ds(src, 2 * rt)], buf_ref.at[s, pl.ds(r * 2 * rt, 2 * rt)],
                                      sem_ref.at[s]).start(priority=u % 2)
            return 0
        lax.fori_loop(0, tc // ISSUE_UNROLL, issue, 0)

    @pl.when(i == 0)
    def _():
        gather(pos_ref, slot)

    @pl.when(i + 1 < n)
    def _():
        gather(nxt_ref, 1 - slot)

    pltpu.make_async_copy(ys_ref.at[pl.ds(0, tc * 2 * rt)], buf_ref.at[slot], sem_ref.at[slot]).wait()

    wc = wc_ref[...]
    r0 = _row_tiles_load(buf_ref.at[slot], tc, D, group=2, member=0)
    r1 = _row_tiles_load(buf_ref.at[slot], tc, D, group=2, member=1)
    y = x1_ref[...] + (wc[:, 0:1] * r0 + wc[:, 1:2] * r1)
    o_ref[...] = _rms(y, g_ref[...])


def _combine(x1, pos, wcol, ys, final_g):
    N, D = x1.shape
    tc = min(N, 256)
    assert N % tc == 0 and tc % ISSUE_UNROLL == 0
    nt = N // tc
    rt = D // LANES
    pos3 = pos.reshape(nt, 1, tc)
    smem = lambda f: pl.BlockSpec((None, 1, tc), f, memory_space=pltpu.SMEM)
    return pl.pallas_call(
        _combine_kernel,
        grid=(nt,),
        in_specs=[smem(lambda i: (i, 0, 0)), smem(lambda i: (jnp.minimum(i + 1, nt - 1), 0, 0)),
                  pl.BlockSpec((tc, D), lambda i: (i, 0)), pl.BlockSpec((tc, LANES), lambda i: (i, 0)),
                  _const_spec((1, D)), pl.BlockSpec(memory_space=pl.ANY)],
        out_specs=pl.BlockSpec((tc, D), lambda i: (i, 0)),
        out_shape=jax.ShapeDtypeStruct((N, D), F32),
        scratch_shapes=[pltpu.VMEM((2, tc * 2 * rt, LANES), F32), pltpu.SemaphoreType.DMA((2,))],
        compiler_params=_params(("arbitrary",)),
        name="combine",
    )(pos3, pos3, x1, wcol, final_g, ys)


def _moe_and_norm(x1, ri, wcol, counts, p):
    N, D = x1.shape
    tf = 512 if N >= 8 * 512 else min(N, 256)
    sched, starts = _schedule(counts, N, tf)
    hit = ri[0][None, :] == jnp.arange(N_CLASSES, dtype=jnp.int32)[:, None]
    pos = jnp.sum(jnp.where(hit, starts.astype(jnp.int32)[:, None], 0), axis=0) + ri[1]
    xs = _dispatch(x1, pos, p["gffn"])
    ys = _ffn(xs, sched, p["w_eg"], p["w_eu"], p["w_ed"], tf)
    return _combine(x1, pos, wcol, ys, p["final_g"])


def _stream(x, pos0, pool_hdr, cache, p, n_heads):
    B, T, D = x.shape
    q, k, v, py, npool = _inproj(x, pool_hdr, pos0, p["gmix"], p["w_a"], p["pool_w"], p["pool_scale"], n_heads)
    if cache is None:
        at = _attn_prompt(q, k, v, p["lam_vecs"], p["subln_g"], n_heads)
    else:
        at = _attn_sample(q, k, v, cache[0], cache[1], p["lam_vecs"], p["subln_g"], n_heads)
    N = B * T
    x1, ri, wcol, cnt = _merge(x.reshape(N, D), py.reshape(N, -1), at.reshape(N, -1), p)
    y = _moe_and_norm(x1, ri, wcol, cnt[:N_CLASSES, 0], p)
    return (y.reshape(B, T, D), k.reshape(1, B, T, n_heads, 2 * HEAD_DIM), v.reshape(1, B, T, n_heads, V_DIM),
            npool[None, :, 1:, :])


def kernel(x_prompt, x_sample, cache_k, cache_v, state_pool, norm_mix_g, w_in, b_gate, lambda_q1, lambda_k1,
           lambda_q2, lambda_k2, subln_g, pool_w, pool_scale, w_pool_branch, w_attn_branch, w_out, norm_ffn_g,
           w_router_group, b_router_group, w_router_expert, b_router_expert, w_expert_gate, w_expert_up,
           w_expert_down, final_norm_g):
    assert w_in.shape[0] == 1, "single layer"
    D = x_prompt.shape[-1]
    n_heads = cache_k.shape[3]
    P = pool_scale.shape[-1]
    past = cache_k.shape[2]
    n_a = P + 2 * n_heads * 2 * HEAD_DIM + n_heads * V_DIM

    wr = jnp.concatenate([w_router_group[0], jnp.transpose(w_router_expert[0], (1, 0, 2)).reshape(D, N_EXPERTS)],
                         axis=1)
    br = jnp.concatenate([b_router_group[0], b_router_expert[0].reshape(N_EXPERTS)])
    n_r = wr.shape[1]
    p = dict(
        gmix=norm_mix_g[0].reshape(1, D),
        w_a=w_in[0, :, :n_a].astype(BF16),
        w_gate=w_in[0, :, n_a:].astype(BF16),
        b_gate=b_gate[0].reshape(1, 2 * D),
        lam_vecs=jnp.stack([lambda_q1[0], lambda_k1[0], lambda_q2[0], lambda_k2[0]]),
        subln_g=subln_g[0],
        pool_w=pool_w[0].astype(BF16),
        pool_scale=pool_scale[0],
        w_pb=w_pool_branch[0].astype(BF16),
        w_ab=w_attn_branch[0].astype(BF16),
        w_out=w_out[0].astype(BF16),
        gffn=norm_ffn_g[0].reshape(1, D),
        w_router=jnp.pad(wr, ((0, 0), (0, LANES - n_r))).astype(BF16),
        b_router=jnp.pad(br, (0, LANES - n_r)).reshape(1, LANES),
        w_eg=w_expert_gate[0].astype(BF16),
        w_eu=w_expert_up[0].astype(BF16),
        w_ed=w_expert_down[0].astype(BF16),
        final_g=final_norm_g.reshape(1, D),
    )

    Bp = x_prompt.shape[0]
    Bs = x_sample.shape[0]
    hdr_p = jnp.zeros((Bp, POOL_HDR, P), F32)
    hdr_s = jnp.pad(state_pool[0], ((0, 0), (1, 0), (0, 0)))
    cache = (cache_k[0].reshape(Bs, past * n_heads, -1), cache_v[0].reshape(Bs, past * n_heads, -1))

    yp, kp, vp, pp = _stream(x_prompt, 0, hdr_p, None, p, n_heads)
    ys, ks, vs, ps = _stream(x_sample, past, hdr_s, cache, p, n_heads)
    return (yp, ys, kp, vp, pp, ks, vs, ps)
```

```python
import functools
import math

import jax
import jax.numpy as jnp
from jax import lax
from jax.experimental import pallas as pl
from jax.experimental.pallas import tpu as pltpu

CHUNK = 64
HEAD_DIM = 64
V_DIM = 2 * HEAD_DIM
ROT_DIM = HEAD_DIM // 4
ROPE_THETA = 500000.0
ATTN_SCALE = HEAD_DIM ** -0.5
Q_SCALE = ATTN_SCALE * math.log2(math.e)
POOL_WINDOWS = (2, 4, 8, 16)
POOL_GROUP_DIM = 128
POOL_BUF = max(POOL_WINDOWS) - 1
POOL_HDR = POOL_BUF + 1
N_EXPERT_GROUPS = 4
EXPERTS_PER_GROUP = 4
N_EXPERTS = N_EXPERT_GROUPS * EXPERTS_PER_GROUP
RMS_EPS = 1e-6
LAM_INIT = 0.8 - 0.6 * math.exp(-0.3 * 0)

LANES = 128
ISSUE_UNROLL = 8
ONES_ROWS = 16
FFN_ROW_CHUNKS = 2
VMEM_LIMIT = 56 * 1024 * 1024
NEG_BIG = -1e30

F32 = jnp.float32
BF16 = jnp.bfloat16


def _params(semantics, fuse=None):
    return pltpu.CompilerParams(dimension_semantics=semantics, vmem_limit_bytes=VMEM_LIMIT,
                                allow_input_fusion=fuse)


def _const_spec(shape):
    nd = len(shape)
    return pl.BlockSpec(shape, lambda *_: (0,) * nd)


def _rms(x, g):
    ms = jnp.mean(x * x, axis=-1, keepdims=True)
    return (x * lax.rsqrt(ms + RMS_EPS)) * g


def _inproj_kernel(x_ref, g_ref, w_ref, cos_ref, sa_ref, sb_ref, pbuf_ref, pw_ref, ps_ref,
                   q_ref, k_ref, v_ref, py_ref, npool_ref, ext_ref, carry_ref, *, pos0, n_heads):
    t = pl.program_id(0)
    b = pl.program_id(1)
    tm = x_ref.shape[0]
    pool_w = ext_ref.shape[1]
    qk_w = n_heads * 2 * HEAD_DIM

    h = _rms(x_ref[...], g_ref[...]).astype(BF16)

    @pl.when(t == 0)
    def _():
        ext_ref[0:POOL_HDR, :] = pbuf_ref[...]

    @pl.when(t > 0)
    def _():
        ext_ref[0:POOL_HDR, :] = carry_ref[b]

    ext_ref[POOL_HDR:, :] = jnp.dot(h, w_ref[:, 0:pool_w], preferred_element_type=F32)
    tail = ext_ref[tm:tm + POOL_HDR, :]
    carry_ref[b] = tail
    npool_ref[...] = tail

    cos = cos_ref[...]
    sa = sa_ref[...]
    sb = sb_ref[...]
    col_chunk = 4 * LANES
    for which, out_ref in ((0, q_ref), (1, k_ref)):
        for c in range(0, qk_w, col_chunk):
            base = pool_w + which * qk_w + c
            z = jnp.dot(h, w_ref[:, base:base + col_chunk], preferred_element_type=F32)
            for j in range(0, col_chunk, LANES):
                zc = z[:, j:j + LANES]
                rot = (zc * cos + pltpu.roll(zc, LANES - ROT_DIM // 2, 1) * sa
                       + pltpu.roll(zc, ROT_DIM // 2, 1) * sb)
                if which == 0:
                    out_ref[:, c + j:c + j + LANES] = (rot * Q_SCALE).astype(BF16)
                else:
                    out_ref[:, c + j:c + j + LANES] = rot

    v_w = v_ref.shape[1]
    for c in range(0, v_w, col_chunk):
        base = pool_w + 2 * qk_w + c
        v_ref[:, c:c + col_chunk] = jnp.dot(h, w_ref[:, base:base + col_chunk], preferred_element_type=F32)

    pos = pos0 + t * tm + lax.broadcasted_iota(jnp.int32, (tm, 1), 0)
    for g, w in enumerate(POOL_WINDOWS):
        sl = slice(g * POOL_GROUP_DIM, (g + 1) * POOL_GROUP_DIM)
        u = ext_ref[POOL_HDR:POOL_HDR + tm, sl]
        ws = u
        for j in range(1, w):
            ws = ws + ext_ref[POOL_HDR - j:POOL_HDR - j + tm, sl]
        cnt = jnp.minimum(w, pos + 1).astype(F32)
        d = (ws / cnt - u).astype(BF16)
        y = jnp.dot(d, pw_ref[g], preferred_element_type=F32) * ps_ref[:, sl]
        py_ref[:, sl] = y.astype(BF16)


def _inproj(x, pool_hdr, pos0, norm_g, w_a, pool_w, pool_scale, n_heads):
    B, T, D = x.shape
    P = pool_scale.shape[-1]
    qk_w = n_heads * 2 * HEAD_DIM
    v_w = n_heads * V_DIM
    tm = min(T, 512)
    assert T % tm == 0 and T >= POOL_HDR and tm % 16 == 0
    nt = T // tm

    pos = pos0 + jnp.arange(T)
    inv_freq = jnp.power(ROPE_THETA, -jnp.arange(0, ROT_DIM, 2, dtype=F32) / ROT_DIM)
    ang = pos.astype(F32)[:, None] * inv_freq[None, :]
    cos, sin = jnp.cos(ang), jnp.sin(ang)
    half = ROT_DIM // 2
    ones = jnp.ones((T, HEAD_DIM - ROT_DIM), F32)
    zeros_h = jnp.zeros((T, half), F32)
    zeros_r = jnp.zeros((T, HEAD_DIM - ROT_DIM), F32)
    cos_t = jnp.tile(jnp.concatenate([cos, cos, ones], axis=1), (1, LANES // HEAD_DIM))
    sa_t = jnp.tile(jnp.concatenate([-sin, zeros_h, zeros_r], axis=1), (1, LANES // HEAD_DIM))
    sb_t = jnp.tile(jnp.concatenate([zeros_h, sin, zeros_r], axis=1), (1, LANES // HEAD_DIM))

    row = lambda w: pl.BlockSpec((None, tm, w), lambda t, b: (b, t, 0))
    tab = pl.BlockSpec((tm, LANES), lambda t, b: (t, 0))
    hdr = pl.BlockSpec((None, POOL_HDR, P), lambda t, b: (b, 0, 0))
    hdr_out = pl.BlockSpec((None, POOL_HDR, P), lambda t, b: (jnp.where(t == nt - 1, b, 0), 0, 0))
    kern = functools.partial(_inproj_kernel, pos0=pos0, n_heads=n_heads)
    return pl.pallas_call(
        kern,
        grid=(nt, B),
        in_specs=[row(D), _const_spec((1, D)), _const_spec(w_a.shape), tab, tab, tab, hdr,
                  _const_spec(pool_w.shape), _const_spec((1, P))],
        out_specs=[row(qk_w), row(qk_w), row(v_w), row(P), hdr_out],
        out_shape=[jax.ShapeDtypeStruct((B, T, qk_w), BF16),
                   jax.ShapeDtypeStruct((B, T, qk_w), F32),
                   jax.ShapeDtypeStruct((B, T, v_w), F32),
                   jax.ShapeDtypeStruct((B, T, P), BF16),
                   jax.ShapeDtypeStruct((B, POOL_HDR, P), F32)],
        scratch_shapes=[pltpu.VMEM((POOL_HDR + tm, P), F32), pltpu.VMEM((B, POOL_HDR, P), F32)],
        compiler_params=_params(("arbitrary", "arbitrary"), fuse=[False, False, True, False, False, False,
                                                                  False, True, False]),
        name="inproj",
    )(x, norm_g.reshape(1, D), w_a, cos_t, sa_t, sb_t, pool_hdr, pool_w, pool_scale.reshape(1, P))


def _lambda(lam_ref):
    lv = lam_ref[...]
    a = jnp.sum(lv[0:1] * lv[1:2], axis=-1, keepdims=True)
    c = jnp.sum(lv[2:3] * lv[3:4], axis=-1, keepdims=True)
    return jnp.exp(a) - jnp.exp(c) + LAM_INIT


def _split_maps(q):
    lane = lax.broadcasted_iota(jnp.int32, q.shape, 1)
    zero = jnp.zeros_like(q)
    return jnp.where(lane < HEAD_DIM, q, zero), jnp.where(lane >= HEAD_DIM, q, zero)


def _nt_dot(a, b):
    return lax.dot_general(a, b, (((1,), (1,)), ((), ())), preferred_element_type=F32)


def _head_out(o, g):
    ms = jnp.mean(o * o, axis=-1, keepdims=True)
    return ((o * lax.rsqrt(ms + RMS_EPS)) * g) * (1.0 - LAM_INIT)


def _attn_prompt_kernel(q_ref, k_ref, v_ref, lam_ref, g_ref, o_ref,
                        kb_ref, vt_ref, qt_ref, acc1_ref, acc2_ref, stat_ref, sa_ref, sb_ref, sc_ref, *, tq, tk):
    T = q_ref.shape[0]
    nq = T // tq
    half = tq // 2
    assert tk == half

    kb_ref[...] = k_ref[...].astype(BF16)
    ones_row = jnp.where(lax.broadcasted_iota(jnp.int32, (ONES_ROWS, tk), 0) == 0, 1.0, 0.0).astype(BF16)
    for c in range(T // tk):
        vt_ref[c, 0:V_DIM, :] = v_ref[c * tk:(c + 1) * tk, :].T.astype(BF16)
        vt_ref[c, V_DIM:, :] = ones_row
    for c in range(nq):
        qt = q_ref[c * tq:(c + 1) * tq, :].astype(F32).T.astype(BF16)
        row = lax.broadcasted_iota(jnp.int32, qt.shape, 0)
        zero = jnp.zeros_like(qt)
        qt_ref[c, 0] = jnp.where(row < HEAD_DIM, qt, zero)
        qt_ref[c, 1] = jnp.where(row >= HEAD_DIM, qt, zero)

    def scores(j, s_ref, i, right_half=False):
        kj = kb_ref[pl.ds(pl.multiple_of(j * tk, tk), tk), :]
        for c in range(2):
            if right_half:
                s_ref[c, :, half:tq] = jnp.dot(kj, qt_ref[i, c, :, half:], preferred_element_type=F32)
            else:
                s_ref[c, :, 0:tq] = jnp.dot(kj, qt_ref[i, c], preferred_element_type=F32)

    def softmax_pv(j, s_ref, kind="full", first=False):
        vtj = vt_ref[j]
        cols = slice(half, tq) if kind == "diag2" else slice(0, tq)
        width = cols.stop - cols.start
        if kind != "full":
            key_chunk = lax.broadcasted_iota(jnp.int32, (tk, width), 0) // CHUNK
            qry_chunk = lax.broadcasted_iota(jnp.int32, (tk, width), 1) // CHUNK
            mask = key_chunk <= qry_chunk
        for c, acc_ref in ((0, acc1_ref), (1, acc2_ref)):
            s = s_ref[c, :, cols]
            if kind != "full":
                s = jnp.where(mask, s, NEG_BIG)
            smax = jnp.max(s, axis=0, keepdims=True)
            if first:
                mn = smax
            else:
                m = stat_ref[c:c + 1, cols]
                mn = jnp.maximum(m, smax)
                alpha = jnp.exp2(m - mn)
            p = jnp.exp2(s - mn)
            stat_ref[c:c + 1, cols] = mn
            pv = jnp.dot(vtj, p.astype(BF16), preferred_element_type=F32)
            if first:
                acc_ref[:, cols] = pv
            else:
                acc_ref[:, cols] = alpha * acc_ref[:, cols] + pv

    lam = _lambda(lam_ref)
    scores(0, sc_ref, 0)

    def q_tile(i, _):
        nxt = jnp.minimum(i + 1, nq - 1)

        @pl.when(i == 0)
        def _():
            scores(1, sa_ref, i, right_half=True)
            softmax_pv(0, sc_ref, "diag1", first=True)
            scores(0, sc_ref, nxt)
            softmax_pv(1, sa_ref, "diag2")

        @pl.when(i > 0)
        def _():
            scores(1, sa_ref, i)
            softmax_pv(0, sc_ref, "full", first=True)

            def pair(p, _):
                t = 2 * p + 1
                scores(t + 1, sb_ref, i)
                softmax_pv(t, sa_ref)
                scores(t + 2, sa_ref, i)
                softmax_pv(t + 1, sb_ref)
                return 0

            lax.fori_loop(0, i - 1, pair, 0)

            d1 = 2 * i
            scores(d1, sb_ref, i)
            softmax_pv(d1 - 1, sa_ref)
            scores(d1 + 1, sa_ref, i, right_half=True)
            softmax_pv(d1, sb_ref, "diag1")
            scores(0, sc_ref, nxt)
            softmax_pv(d1 + 1, sa_ref, "diag2")

        l1 = acc1_ref[V_DIM:V_DIM + 1, :]
        l2 = acc2_ref[V_DIM:V_DIM + 1, :]
        o = acc1_ref[0:V_DIM, :] / l1 - lam * (acc2_ref[0:V_DIM, :] / l2)
        ms = jnp.mean(o * o, axis=0, keepdims=True)
        o = ((o * lax.rsqrt(ms + RMS_EPS)) * g_ref[...]) * (1.0 - LAM_INIT)
        o_ref[pl.ds(pl.multiple_of(i * tq, tq), tq), :] = o.T.astype(BF16)
        return 0

    lax.fori_loop(0, nq, q_tile, 0)


def _attn_prompt(q, k, v, lam_vecs, subln_g, n_heads):
    B, T, _ = q.shape
    tq = min(T, 1024)
    tk = tq // 2
    assert T % tq == 0 and tk % LANES == 0 and tk % CHUNK == 0
    blk = pl.BlockSpec((None, T, V_DIM), lambda b, h: (b, 0, h))
    score_buf = pltpu.VMEM((2, tk, tq + LANES), F32)
    acc_buf = pltpu.VMEM((V_DIM + ONES_ROWS, tq), F32)
    return pl.pallas_call(
        functools.partial(_attn_prompt_kernel, tq=tq, tk=tk),
        grid=(B, n_heads),
        in_specs=[blk, blk, blk, _const_spec(lam_vecs.shape), _const_spec((V_DIM, 1))],
        out_specs=blk,
        out_shape=jax.ShapeDtypeStruct((B, T, n_heads * V_DIM), BF16),
        scratch_shapes=[pltpu.VMEM((T, V_DIM), BF16), pltpu.VMEM((T // tk, V_DIM + ONES_ROWS, tk), BF16),
                        pltpu.VMEM((T // tq, 2, V_DIM, tq), BF16), acc_buf, acc_buf,
                        pltpu.VMEM((2, tq), F32), score_buf, score_buf, score_buf],
        compiler_params=_params(("arbitrary", "arbitrary")),
        name="attn_prompt",
    )(q, k, v, lam_vecs, subln_g.reshape(V_DIM, 1))


def _attn_sample_kernel(q_ref, ck_ref, cv_ref, nk_ref, nv_ref, lam_ref, g_ref, o_ref, *, past, n_heads):
    S = q_ref.shape[0]
    P = past
    q_chunk = (past + lax.broadcasted_iota(jnp.int32, (S, 1), 0)) // CHUNK
    mask_c = (lax.broadcasted_iota(jnp.int32, (S, P), 1) // CHUNK) <= q_chunk
    mask_n = ((past + lax.broadcasted_iota(jnp.int32, (S, S), 1)) // CHUNK) <= q_chunk
    lam = _lambda(lam_ref)
    for h in range(n_heads):
        cols = slice(h * V_DIM, (h + 1) * V_DIM)
        q1, q2 = _split_maps(q_ref[:, cols])
        ck = ck_ref[pl.ds(h, P, stride=n_heads), :].astype(BF16)
        cv = cv_ref[pl.ds(h, P, stride=n_heads), :].astype(BF16)
        nk = nk_ref[:, cols].astype(BF16)
        nv = nv_ref[:, cols].astype(BF16)
        outs = []
        for qm in (q1, q2):
            sc = jnp.where(mask_c, _nt_dot(qm, ck), NEG_BIG)
            sn = jnp.where(mask_n, _nt_dot(qm, nk), NEG_BIG)
            m = jnp.maximum(jnp.max(sc, axis=-1, keepdims=True), jnp.max(sn, axis=-1, keepdims=True))
            pc = jnp.exp2(sc - m)
            pn = jnp.exp2(sn - m)
            l = jnp.sum(pc, axis=-1, keepdims=True) + jnp.sum(pn, axis=-1, keepdims=True)
            acc = (jnp.dot(pc.astype(BF16), cv, preferred_element_type=F32)
                   + jnp.dot(pn.astype(BF16), nv, preferred_element_type=F32))
            outs.append(acc / l)
        o = outs[0] - lam * outs[1]
        o_ref[:, cols] = _head_out(o, g_ref[...]).astype(BF16)


def _attn_sample(q, k, v, cache_k, cache_v, lam_vecs, subln_g, n_heads):
    B, S, W = q.shape
    P = cache_k.shape[1] // n_heads
    nblk = pl.BlockSpec((None, S, W), lambda b: (b, 0, 0))
    cblk = pl.BlockSpec((None, P * n_heads, V_DIM), lambda b: (b, 0, 0))
    kern = functools.partial(_attn_sample_kernel, past=P, n_heads=n_heads)
    return pl.pallas_call(
        kern,
        grid=(B,),
        in_specs=[nblk, cblk, cblk, nblk, nblk, _const_spec(lam_vecs.shape), _const_spec((1, V_DIM))],
        out_specs=nblk,
        out_shape=jax.ShapeDtypeStruct((B, S, W), BF16),
        compiler_params=_params(("arbitrary",)),
        name="attn_sample",
    )(q, cache_k, cache_v, k, v, lam_vecs, subln_g.reshape(1, V_DIM))


def _merge_kernel(x_ref, py_ref, at_ref, gmix_ref, wg_ref, bg_ref, wpb_ref, wab_ref, wo_ref,
                  gffn_ref, wr_ref, br_ref, tri_ref,
                  x1_ref, ri_ref, wc_ref, cnt_ref, wrow_ref, run_ref):
    step = pl.program_id(0)
    tm, D = x_ref.shape

    @pl.when(step == 0)
    def _():
        wrow_ref[...] = jnp.zeros_like(wrow_ref)
        run_ref[...] = jnp.zeros_like(run_ref)

    x = x_ref[...]
    h = _rms(x, gmix_ref[...]).astype(BF16)
    gl = jnp.dot(h, wg_ref[...], preferred_element_type=F32) + bg_ref[...]
    a = jnp.dot(py_ref[...], wpb_ref[...], preferred_element_type=F32)
    bb = jnp.dot(at_ref[...], wab_ref[...], preferred_element_type=F32)
    merged = jax.nn.sigmoid(gl[:, :D]) * a + jax.nn.sigmoid(gl[:, D:]) * bb
    x1 = x + jnp.dot(merged.astype(BF16), wo_ref[...], preferred_element_type=F32)
    x1_ref[...] = x1

    h2 = _rms(x1, gffn_ref[...]).astype(BF16)
    logits = jnp.dot(h2, wr_ref[...], preferred_element_type=F32) + br_ref[...]
    lt = logits.T
    G = N_EXPERT_GROUPS
    E = EXPERTS_PER_GROUP
    best = lt[0:1]
    gsel = jnp.zeros((1, tm), jnp.int32)
    for g in range(1, G):
        better = lt[g:g + 1] > best
        gsel = jnp.where(better, g, gsel)
        best = jnp.where(better, lt[g:g + 1], best)
    gden = jnp.zeros((1, tm), F32)
    for g in range(G):
        gden = gden + jnp.exp(lt[g:g + 1] - best)
    g_w = 1.0 / gden
    el = [lt[G + e:G + e + 1] for e in range(E)]
    for g in range(1, G):
        sel = gsel == g
        el = [jnp.where(sel, lt[G + g * E + e:G + g * E + e + 1], el[e]) for e in range(E)]
    v0 = el[0]
    i0 = jnp.zeros((1, tm), jnp.int32)
    for e in range(1, E):
        better = el[e] > v0
        i0 = jnp.where(better, e, i0)
        v0 = jnp.where(better, el[e], v0)
    v1 = jnp.full((1, tm), -jnp.inf, F32)
    i1 = jnp.zeros((1, tm), jnp.int32)
    for e in range(E):
        better = (i0 != e) & (el[e] > v1)
        i1 = jnp.where(better, e, i1)
        v1 = jnp.where(better, el[e], v1)
    ex = jnp.exp(v1 - v0)
    w0 = (1.0 / (1.0 + ex)) * g_w
    w1 = (ex / (1.0 + ex)) * g_w
    e0 = gsel * E + i0
    e1 = gsel * E + i1

    eid = lax.broadcasted_iota(jnp.int32, (N_EXPERTS, tm), 0)
    oh0 = eid == e0
    oh1 = eid == e1
    oh = jnp.where(oh0 | oh1, 1.0, 0.0).astype(BF16)
    pref = jnp.dot(oh, tri_ref[...], preferred_element_type=F32)
    rank = run_ref[...] + pref - 1.0
    r0 = jnp.sum(jnp.where(oh0, rank, 0.0), axis=0, keepdims=True)
    r1 = jnp.sum(jnp.where(oh1, rank, 0.0), axis=0, keepdims=True)
    run_ref[...] = run_ref[...] + pref[:, tm - 1:tm]
    cnt_ref[...] = jnp.broadcast_to(run_ref[...], cnt_ref.shape).astype(jnp.int32)

    ri_ref[0:1, :] = e0
    ri_ref[1:2, :] = e1
    ri_ref[2:3, :] = r0.astype(jnp.int32)
    ri_ref[3:4, :] = r1.astype(jnp.int32)
    wrow_ref[0:1, :] = w0
    wrow_ref[1:2, :] = w1
    wc_ref[...] = wrow_ref[...].T


def _merge(x, py, at, p):
    N, D = x.shape
    tm = min(N, 512)
    assert N % tm == 0
    tri = (jnp.arange(tm)[:, None] <= jnp.arange(tm)[None, :]).astype(BF16)
    row = lambda w: pl.BlockSpec((tm, w), lambda i: (i, 0))
    consts = [p["gmix"], p["w_gate"], p["b_gate"], p["w_pb"], p["w_ab"], p["w_out"],
              p["gffn"], p["w_router"], p["b_router"], tri]
    return pl.pallas_call(
        _merge_kernel,
        grid=(N // tm,),
        in_specs=[row(D), row(py.shape[1]), row(at.shape[1])] + [_const_spec(c.shape) for c in consts],
        out_specs=[row(D), pl.BlockSpec((4, tm), lambda i: (0, i)), row(LANES),
                   _const_spec((N_EXPERTS, LANES))],
        out_shape=[jax.ShapeDtypeStruct((N, D), F32),
                   jax.ShapeDtypeStruct((4, N), jnp.int32),
                   jax.ShapeDtypeStruct((N, LANES), F32),
                   jax.ShapeDtypeStruct((N_EXPERTS, LANES), jnp.int32)],
        scratch_shapes=[pltpu.VMEM((LANES, tm), F32), pltpu.VMEM((N_EXPERTS, 1), F32)],
        compiler_params=_params(("arbitrary",), fuse=[False] * 3 + [False, True, False, True, True, True,
                                                               False, False, False, False]),
        name="merge",
    )(x, py, at, *consts)


def _row_tiles_store(ref_view, x, row0=0):
    rows = x.shape[0]
    n = x.shape[1] // LANES
    for s in range(n):
        ref_view[pl.ds(row0 * n + s, rows, stride=n), :] = x[:, s * LANES:(s + 1) * LANES]


def _row_tiles_load(ref_view, rows, width, row0=0):
    n = width // LANES
    return jnp.concatenate([ref_view[pl.ds(row0 * n + s, rows, stride=n), :] for s in range(n)], axis=1)


def _dispatch_kernel(pos_ref, x1_ref, g_ref, xs_ref, buf_ref, sem_ref):
    i = pl.program_id(0)
    n = pl.num_programs(0)
    td, D = x1_ref.shape
    rt = D // LANES
    slot = i % 2

    def drain(s):
        for _ in range(2):
            pltpu.make_async_copy(buf_ref.at[s], xs_ref.at[pl.ds(0, td * rt)], sem_ref.at[s]).wait()

    @pl.when(i >= 2)
    def _():
        drain(slot)

    _row_tiles_store(buf_ref.at[slot], _rms(x1_ref[...], g_ref[...]))

    def issue(r8, _):
        for u in range(ISSUE_UNROLL):
            r = r8 * ISSUE_UNROLL + u
            for c in range(2):
                dst = pl.multiple_of(pos_ref[0, c * td + r] * rt, rt)
                pltpu.make_async_copy(buf_ref.at[slot, pl.ds(r * rt, rt)], xs_ref.at[pl.ds(dst, rt)],
                                      sem_ref.at[slot]).start(priority=c)
        return 0

    lax.fori_loop(0, td // ISSUE_UNROLL, issue, 0)

    @pl.when(i == n - 1)
    def _():
        drain(slot)

        @pl.when(n >= 2)
        def _():
            drain(1 - slot)


def _dispatch(x1, pos, gffn):
    N, D = x1.shape
    td = min(N, 1024)
    assert N % td == 0 and td % ISSUE_UNROLL == 0
    nt = N // td
    rt = D // LANES
    pos3 = pos.reshape(2, nt, td).transpose(1, 0, 2).reshape(nt, 1, 2 * td)
    return pl.pallas_call(
        _dispatch_kernel,
        grid=(nt,),
        in_specs=[pl.BlockSpec((None, 1, 2 * td), lambda i: (i, 0, 0), memory_space=pltpu.SMEM),
                  pl.BlockSpec((td, D), lambda i: (i, 0)), _const_spec((1, D))],
        out_specs=pl.BlockSpec(memory_space=pl.ANY),
        out_shape=jax.ShapeDtypeStruct((2 * N * rt, LANES), F32),
        scratch_shapes=[pltpu.VMEM((2, td * rt, LANES), F32), pltpu.SemaphoreType.DMA((2,))],
        compiler_params=_params(("arbitrary",)),
        name="dispatch",
    )(pos3, x1, gffn)


def _ffn_kernel(tile_ref, ex_ref, lo_ref, hi_ref, first_ref, xs_ref, wg_ref, wu_ref, wd_ref, ys_ref):
    i = pl.program_id(0)
    D = wg_ref.shape[0]
    rt = D // LANES
    tf = xs_ref.shape[0] // rt
    lo = lo_ref[i]
    hi = hi_ref[i]

    @pl.when(first_ref[i] == 1)
    def _():
        ys_ref[...] = jnp.zeros_like(ys_ref)

    @pl.when(hi > lo)
    def _():
        rc = tf // FFN_ROW_CHUNKS
        for ch in range(FFN_ROW_CHUNKS):
            r0 = ch * rc
            xb = _row_tiles_load(xs_ref, rc, D, r0).astype(BF16)
            gate = jnp.dot(xb, wg_ref[...], preferred_element_type=F32)
            up = jnp.dot(xb, wu_ref[...], preferred_element_type=F32)
            hid = (jax.nn.silu(gate) * up).astype(BF16)
            y = jnp.dot(hid, wd_ref[...], preferred_element_type=F32)
            rows = tile_ref[i] * tf + r0 + lax.broadcasted_iota(jnp.int32, (rc, 1), 0)
            mine = (rows >= lo) & (rows < hi)
            _row_tiles_store(ys_ref, jnp.where(mine, y, _row_tiles_load(ys_ref, rc, D, r0)), r0)


def _ffn(xs, sched, w_gate, w_up, w_down, tf):
    D, Hd = w_gate.shape[1:]
    rt = D // LANES
    n_items = sched[0].shape[0]
    rows = pl.BlockSpec((tf * rt, LANES), lambda i, tile, ex, lo, hi, first: (tile[i], 0))
    wspec = lambda a, b_: pl.BlockSpec((None, a, b_), lambda i, tile, ex, lo, hi, first: (ex[i], 0, 0))
    return pl.pallas_call(
        _ffn_kernel,
        grid_spec=pltpu.PrefetchScalarGridSpec(
            num_scalar_prefetch=5,
            grid=(n_items,),
            in_specs=[rows, wspec(D, Hd), wspec(D, Hd), wspec(Hd, D)],
            out_specs=rows,
        ),
        out_shape=jax.ShapeDtypeStruct(xs.shape, F32),
        compiler_params=_params(("arbitrary",), fuse=[False] * 6 + [True] * 3),
        name="expert_ffn",
    )(*sched, xs, w_gate, w_up, w_down)


def _schedule(counts, M, tf):
    n_tiles = M // tf
    n_items = n_tiles + N_EXPERTS - 1
    ends = jnp.cumsum(counts)
    starts = ends - counts
    first_tile = starts // tf
    last_tile = jnp.maximum(ends - 1, 0) // tf
    per = jnp.where(counts > 0, last_tile - first_tile + 1, 0)
    item_end = jnp.cumsum(per)
    item_start = item_end - per
    total = item_end[-1]
    idx = jnp.arange(n_items, dtype=jnp.int32)
    valid = idx < total
    idx_c = jnp.minimum(idx, total - 1)
    ex = jnp.minimum(jnp.sum(item_end[None, :] <= idx_c[:, None], axis=1), N_EXPERTS - 1).astype(jnp.int32)
    onehot = ex[:, None] == jnp.arange(N_EXPERTS)[None, :]
    pick = lambda table: jnp.sum(jnp.where(onehot, table[None, :], 0), axis=1)
    tile = (pick(first_tile) + idx_c - pick(item_start)).astype(jnp.int32)
    lo = jnp.where(valid, pick(starts), 0).astype(jnp.int32)
    hi = jnp.where(valid, pick(ends), 0).astype(jnp.int32)
    prev = jnp.concatenate([jnp.full((1,), -1, jnp.int32), tile[:-1]])
    first = (valid & (tile != prev)).astype(jnp.int32)
    return (tile, ex, lo, hi, first), starts


def _combine_kernel(pos_ref, nxt_ref, x1_ref, wc_ref, g_ref, ys_ref, o_ref, buf_ref, sem_ref):
    i = pl.program_id(0)
    n = pl.num_programs(0)
    tc, D = x1_ref.shape
    rt = D // LANES
    slot = i % 2

    def gather(idx_ref, s):
        def issue(r8, _):
            for u in range(ISSUE_UNROLL):
                r = r8 * ISSUE_UNROLL + u
                for c in range(2):
                    src = pl.multiple_of(idx_ref[0, c * tc + r] * rt, rt)
                    pltpu.make_async_copy(ys_ref.at[pl.ds(src, rt)], buf_ref.at[s, c, pl.ds(r * rt, rt)],
                                          sem_ref.at[s]).start(priority=c)
            return 0
        lax.fori_loop(0, tc // ISSUE_UNROLL, issue, 0)

    @pl.when(i == 0)
    def _():
        gather(pos_ref, slot)

    @pl.when(i + 1 < n)
    def _():
        gather(nxt_ref, 1 - slot)

    for c in range(2):
        pltpu.make_async_copy(ys_ref.at[pl.ds(0, tc * rt)], buf_ref.at[slot, c], sem_ref.at[slot]).wait()

    wc = wc_ref[...]
    r0 = _row_tiles_load(buf_ref.at[slot, 0], tc, D)
    r1 = _row_tiles_load(buf_ref.at[slot, 1], tc, D)
    y = x1_ref[...] + (wc[:, 0:1] * r0 + wc[:, 1:2] * r1)
    o_ref[...] = _rms(y, g_ref[...])


def _combine(x1, pos, wcol, ys, final_g):
    N, D = x1.shape
    tc = min(N, 256)
    assert N % tc == 0 and tc % ISSUE_UNROLL == 0
    nt = N // tc
    rt = D // LANES
    pos3 = pos.reshape(2, nt, tc).transpose(1, 0, 2).reshape(nt, 1, 2 * tc)
    smem = lambda f: pl.BlockSpec((None, 1, 2 * tc), f, memory_space=pltpu.SMEM)
    return pl.pallas_call(
        _combine_kernel,
        grid=(nt,),
        in_specs=[smem(lambda i: (i, 0, 0)), smem(lambda i: (jnp.minimum(i + 1, nt - 1), 0, 0)),
                  pl.BlockSpec((tc, D), lambda i: (i, 0)), pl.BlockSpec((tc, LANES), lambda i: (i, 0)),
                  _const_spec((1, D)), pl.BlockSpec(memory_space=pl.ANY)],
        out_specs=pl.BlockSpec((tc, D), lambda i: (i, 0)),
        out_shape=jax.ShapeDtypeStruct((N, D), F32),
        scratch_shapes=[pltpu.VMEM((2, 2, tc * rt, LANES), F32), pltpu.SemaphoreType.DMA((2,))],
        compiler_params=_params(("arbitrary",)),
        name="combine",
    )(pos3, pos3, x1, wcol, final_g, ys)


def _moe_and_norm(x1, ri, wcol, counts, p):
    N, D = x1.shape
    tf = min(2 * N, 512)
    sched, starts = _schedule(counts, 2 * N, tf)
    hit = ri[0:2][:, None, :] == jnp.arange(N_EXPERTS, dtype=jnp.int32)[None, :, None]
    pos = jnp.sum(jnp.where(hit, starts.astype(jnp.int32)[None, :, None], 0), axis=1) + ri[2:4]
    xs = _dispatch(x1, pos, p["gffn"])
    ys = _ffn(xs, sched, p["w_eg"], p["w_eu"], p["w_ed"], tf)
    return _combine(x1, pos, wcol, ys, p["final_g"])


def _stream(x, pos0, pool_hdr, cache, p, n_heads):
    B, T, D = x.shape
    q, k, v, py, npool = _inproj(x, pool_hdr, pos0, p["gmix"], p["w_a"], p["pool_w"], p["pool_scale"], n_heads)
    if cache is None:
        at = _attn_prompt(q, k, v, p["lam_vecs"], p["subln_g"], n_heads)
    else:
        at = _attn_sample(q, k, v, cache[0], cache[1], p["lam_vecs"], p["subln_g"], n_heads)
    N = B * T
    x1, ri, wcol, cnt = _merge(x.reshape(N, D), py.reshape(N, -1), at.reshape(N, -1), p)
    y = _moe_and_norm(x1, ri, wcol, cnt[:, 0], p)
    return (y.reshape(B, T, D), k.reshape(1, B, T, n_heads, 2 * HEAD_DIM), v.reshape(1, B, T, n_heads, V_DIM),
            npool[None, :, 1:, :])


def kernel(x_prompt, x_sample, cache_k, cache_v, state_pool, norm_mix_g, w_in, b_gate, lambda_q1, lambda_k1,
           lambda_q2, lambda_k2, subln_g, pool_w, pool_scale, w_pool_branch, w_attn_branch, w_out, norm_ffn_g,
           w_router_group, b_router_group, w_router_expert, b_router_expert, w_expert_gate, w_expert_up,
           w_expert_down, final_norm_g):
    assert w_in.shape[0] == 1, "single layer"
    D = x_prompt.shape[-1]
    n_heads = cache_k.shape[3]
    P = pool_scale.shape[-1]
    past = cache_k.shape[2]
    n_a = P + 2 * n_heads * 2 * HEAD_DIM + n_heads * V_DIM

    wr = jnp.concatenate([w_router_group[0], jnp.transpose(w_router_expert[0], (1, 0, 2)).reshape(D, N_EXPERTS)],
                         axis=1)
    br = jnp.concatenate([b_router_group[0], b_router_expert[0].reshape(N_EXPERTS)])
    n_r = wr.shape[1]
    p = dict(
        gmix=norm_mix_g[0].reshape(1, D),
        w_a=w_in[0, :, :n_a].astype(BF16),
        w_gate=w_in[0, :, n_a:].astype(BF16),
        b_gate=b_gate[0].reshape(1, 2 * D),
        lam_vecs=jnp.stack([lambda_q1[0], lambda_k1[0], lambda_q2[0], lambda_k2[0]]),
        subln_g=subln_g[0],
        pool_w=pool_w[0].astype(BF16),
        pool_scale=pool_scale[0],
        w_pb=w_pool_branch[0].astype(BF16),
        w_ab=w_attn_branch[0].astype(BF16),
        w_out=w_out[0].astype(BF16),
        gffn=norm_ffn_g[0].reshape(1, D),
        w_router=jnp.pad(wr, ((0, 0), (0, LANES - n_r))).astype(BF16),
        b_router=jnp.pad(br, (0, LANES - n_r)).reshape(1, LANES),
        w_eg=w_expert_gate[0].astype(BF16),
        w_eu=w_expert_up[0].astype(BF16),
        w_ed=w_expert_down[0].astype(BF16),
        final_g=final_norm_g.reshape(1, D),
    )

    Bp = x_prompt.shape[0]
    Bs = x_sample.shape[0]
    hdr_p = jnp.zeros((Bp, POOL_HDR, P), F32)
    hdr_s = jnp.pad(state_pool[0], ((0, 0), (1, 0), (0, 0)))
    cache = (cache_k[0].reshape(Bs, past * n_heads, -1), cache_v[0].reshape(Bs, past * n_heads, -1))

    yp, kp, vp, pp = _stream(x_prompt, 0, hdr_p, None, p, n_heads)
    ys, ks, vs, ps = _stream(x_sample, past, hdr_s, cache, p, n_heads)
    return (yp, ys, kp, vp, pp, ks, vs, ps)
```
